```python
import math
import jax, jax.numpy as jnp
from jax import lax
import numpy as np

D_MODEL = 1024
BATCH = 8
SEQ = 2048
DEPTH = 2
DEC_BATCH = 128
DEC_SEQ = 8
PAST_LEN = 16384
PAGE_SIZE = 128

N_META = 16
C_CONV = D_MODEL
CONF_W = 31
D_INNER = 2 * D_MODEL
M_HEADDIM = 64
M_HEADS = D_INNER // M_HEADDIM
M_GROUPS = 8
HPG = M_HEADS // M_GROUPS
M_STATE = 128
M_CONV_W = 4
CONV_DIM = D_INNER + 2 * M_GROUPS * M_STATE
SSD_CHUNK = 128
N_IN = 2 * C_CONV + D_INNER + CONV_DIM + M_HEADS + 2 * D_MODEL
PEER_HEADS = 8
N_KEYS = 128
N_EXPERTS = N_KEYS * N_KEYS
PEER_TOPK = 16
D_KEY = 256
PEER_BLOCK = 256
EPS = 1e-6

kernel_name = "hybrid_conformer_ssd_peer_step"


def rmsnorm(x, w):
    xf = x.astype(jnp.float32)
    y = xf * lax.rsqrt(jnp.mean(xf * xf, axis=-1, keepdims=True) + EPS)
    return (y * w.astype(jnp.float32)).astype(x.dtype)


def layernorm(x, g, b):
    xf = x.astype(jnp.float32)
    mu = jnp.mean(xf, axis=-1, keepdims=True)
    xc = xf - mu
    y = xc * lax.rsqrt(jnp.mean(xc * xc, axis=-1, keepdims=True) + EPS)
    return (y * g.astype(jnp.float32) + b.astype(jnp.float32)).astype(x.dtype)


def gated_group_rmsnorm(y, z, w):
    yf = y.astype(jnp.float32) * jax.nn.silu(z.astype(jnp.float32))
    yg = yf.reshape(*y.shape[:-1], M_GROUPS, D_INNER // M_GROUPS)
    yg = yg * lax.rsqrt(jnp.mean(yg * yg, axis=-1, keepdims=True) + EPS)
    return (yg.reshape(y.shape) * w.astype(jnp.float32)).astype(y.dtype)


def causal_depthwise(x_cat, w, b):
    c = x_cat.shape[-1]
    y = lax.conv_general_dilated(x_cat, w[:, None, :].astype(x_cat.dtype), (1,), 'VALID',
                                 dimension_numbers=('NWC', 'WIO', 'NWC'), feature_group_count=c)
    return y + b.astype(y.dtype)


def ssd_segment(x, dt, A, Bm, Cm, h, chunk):
    f32 = jnp.float32
    b, T = x.shape[:2]
    nc = T // chunk
    xf = x.astype(f32).reshape(b, nc, chunk, M_GROUPS, HPG, M_HEADDIM)
    dtf = dt.astype(f32).reshape(b, nc, chunk, M_GROUPS, HPG)
    Bf = Bm.astype(f32).reshape(b, nc, chunk, M_GROUPS, M_STATE)
    Cf = Cm.astype(f32).reshape(b, nc, chunk, M_GROUPS, M_STATE)
    Ag = A.astype(f32).reshape(M_GROUPS, HPG)
    causal = jnp.tril(jnp.ones((chunk, chunk), dtype=bool))
    h0 = h.astype(f32).reshape(b, M_GROUPS, HPG, M_HEADDIM, M_STATE)

    def step(hc, inp):
        xc, dtc, Bc, Cc = inp
        acs = jnp.cumsum(dtc * Ag, axis=1)
        seg = acs[:, :, None] - acs[:, None, :]
        decay = jnp.exp(jnp.where(causal[None, :, :, None, None], seg, -jnp.inf))
        cb = jnp.einsum('btgn,bsgn->btsg', Cc, Bc)
        dx = dtc[..., None] * xc
        y_diag = jnp.einsum('btsgr,bsgrp->btgrp', cb[..., None] * decay, dx)
        y_off = jnp.einsum('btgn,bgrpn->btgrp', Cc, hc) * jnp.exp(acs)[..., None]
        last = acs[:, -1]
        w_end = jnp.exp(last[:, None] - acs)
        st = jnp.einsum('bsgn,bsgrp->bgrpn', Bc, w_end[..., None] * dx)
        h_new = jnp.exp(last)[..., None, None] * hc + st
        return h_new, y_diag + y_off

    inp = tuple(jnp.moveaxis(a, 1, 0) for a in (xf, dtf, Bf, Cf))
    hT, ys = lax.scan(step, h0, inp)
    y = jnp.moveaxis(ys, 0, 1).reshape(b, T, M_HEADS, M_HEADDIM)
    return y.astype(x.dtype), hT.reshape(h.shape).astype(h.dtype)


def peer_block(xb, w_q, keys, U, V):
    T = xb.shape[0]
    q = (xb @ w_q).reshape(T, PEER_HEADS, 2, D_KEY // 2)
    s = jnp.einsum('thcd,hckd->thck', q, keys).astype(jnp.float32)
    s1, i1 = lax.top_k(s[:, :, 0], PEER_TOPK)
    s2, i2 = lax.top_k(s[:, :, 1], PEER_TOPK)
    cand = (s1[..., :, None] + s2[..., None, :]).reshape(T, PEER_HEADS, PEER_TOPK * PEER_TOPK)
    sc, f = lax.top_k(cand, PEER_TOPK)
    ids = (jnp.take_along_axis(i1, f // PEER_TOPK, axis=-1) * N_KEYS
           + jnp.take_along_axis(i2, f % PEER_TOPK, axis=-1))
    g = jax.nn.softmax(sc, axis=-1)
    act = jax.nn.gelu(jnp.einsum('thkd,td->thk', U[ids], xb).astype(jnp.float32), approximate=False)
    w = (g * act).astype(xb.dtype)
    return jnp.einsum('thk,thkd->td', w, V[ids])


def peer(xn, w_q, keys, U, V):
    b, T, D = xn.shape
    n = b * T
    pad = (-n) % PEER_BLOCK
    xt = jnp.pad(xn.reshape(n, D), ((0, pad), (0, 0)))
    out = lax.map(lambda xb: peer_block(xb, w_q, keys, U, V), xt.reshape(-1, PEER_BLOCK, D))
    return out.reshape(-1, D)[:n].reshape(b, T, D)


def trunk_layer(x, conf_buf, mconv_buf, h0, segments, p):
    (norm1_w, w_in, b_in, conf_dw_w, conf_dw_b, conf_ln_g, conf_ln_b, w_conf_out, b_conf_out,
     m_conv_w, m_conv_b, dt_bias, A_log, D_skip, m_norm_w, w_m_out, w_o, norm2_w,
     peer_w_q, peer_keys, peer_u, peer_v) = p
    b, T, _ = x.shape
    xn = rmsnorm(x, norm1_w)
    proj = xn @ w_in + b_in
    cuts = np.cumsum([2 * C_CONV, D_INNER, CONV_DIM, M_HEADS])
    glu_in, z, xbc, dt_raw, gate_raw = jnp.split(proj, [int(c) for c in cuts], axis=-1)

    ga, gb = jnp.split(glu_in, 2, axis=-1)
    u = ga * jax.nn.sigmoid(gb)
    conf_cat = jnp.concatenate([conf_buf.astype(u.dtype), u], axis=1)
    new_conf = conf_cat[:, -(CONF_W - 1):]
    c = causal_depthwise(conf_cat, conf_dw_w, conf_dw_b)
    c = jax.nn.silu(layernorm(c, conf_ln_g, conf_ln_b))
    y_a = c @ w_conf_out + b_conf_out

    m_cat = jnp.concatenate([mconv_buf.astype(xbc.dtype), xbc], axis=1)
    new_mconv = m_cat[:, -(M_CONV_W - 1):]
    xbc_c = jax.nn.silu(causal_depthwise(m_cat, m_conv_w, m_conv_b))
    xs, Bm, Cm = jnp.split(xbc_c, [D_INNER, D_INNER + M_GROUPS * M_STATE], axis=-1)
    xs = xs.reshape(b, T, M_HEADS, M_HEADDIM)
    Bm = Bm.reshape(b, T, M_GROUPS, M_STATE)
    Cm = Cm.reshape(b, T, M_GROUPS, M_STATE)
    dt = jax.nn.softplus(dt_raw.astype(jnp.float32) + dt_bias.astype(jnp.float32))
    A = -jnp.exp(A_log.astype(jnp.float32))
    ys, h, off = [], h0, 0
    for length, chunk in segments:
        y_seg, h = ssd_segment(xs[:, off:off + length], dt[:, off:off + length], A,
                               Bm[:, off:off + length], Cm[:, off:off + length], h, chunk)
        ys.append(y_seg)
        off += length
    y = jnp.concatenate(ys, axis=1) + D_skip[:, None].astype(xs.dtype) * xs
    y = gated_group_rmsnorm(y.reshape(b, T, D_INNER), z, m_norm_w)
    y_b = y @ w_m_out

    g_a, g_b = jnp.split(jax.nn.sigmoid(gate_raw), 2, axis=-1)
    x = x + (g_a * y_a + g_b * y_b) @ w_o

    x = x + peer(rmsnorm(x, norm2_w), peer_w_q, peer_keys, peer_u, peer_v)
    return x, new_conf, new_mconv, h


def setup_inputs(seed: int = 0) -> dict:
    key = jax.random.key(seed)
    ks = jax.random.split(key, 32)
    nrm = lambda k, shape, s: jax.random.normal(k, shape, jnp.float32) * s
    dt0 = jnp.exp(jax.random.uniform(ks[14], (DEPTH, M_HEADS), jnp.float32, math.log(1e-3), math.log(1e-1)))
    return {
        "x_prompt": nrm(ks[0], (BATCH, SEQ, D_MODEL), 1.0),
        "x_sample": nrm(ks[1], (DEC_BATCH, DEC_SEQ, D_MODEL), 1.0),
        "cache_conf": nrm(ks[2], (DEPTH, DEC_BATCH, CONF_W - 1, C_CONV), 0.5),
        "cache_mconv": nrm(ks[3], (DEPTH, DEC_BATCH, M_CONV_W - 1, CONV_DIM), 0.5),
        "state_ssm": nrm(ks[4], (DEPTH, DEC_BATCH, M_HEADS, M_HEADDIM, M_STATE), 0.1),
        "meta_tokens": nrm(ks[5], (N_META, D_MODEL), 1.0),
        "norm1_w": 1.0 + nrm(ks[6], (DEPTH, D_MODEL), 0.02),
        "w_in": nrm(ks[7], (DEPTH, D_MODEL, N_IN), D_MODEL ** -0.5),
        "b_in": nrm(ks[8], (DEPTH, N_IN), 0.02),
        "conf_dw_w": nrm(ks[9], (DEPTH, CONF_W, C_CONV), CONF_W ** -0.5),
        "conf_dw_b": nrm(ks[10], (DEPTH, C_CONV), 0.02),
        "conf_ln_g": 1.0 + nrm(ks[11], (DEPTH, C_CONV), 0.02),
        "conf_ln_b": nrm(ks[12], (DEPTH, C_CONV), 0.02),
        "w_conf_out": nrm(ks[13], (DEPTH, C_CONV, D_MODEL), C_CONV ** -0.5),
        "b_conf_out": nrm(ks[15], (DEPTH, D_MODEL), 0.02),
        "m_conv_w": nrm(ks[16], (DEPTH, M_CONV_W, CONV_DIM), M_CONV_W ** -0.5),
        "m_conv_b": nrm(ks[17], (DEPTH, CONV_DIM), 0.02),
        "dt_bias": dt0 + jnp.log(-jnp.expm1(-dt0)),
        "A_log": jnp.log(jax.random.uniform(ks[18], (DEPTH, M_HEADS), jnp.float32, 1.0, 16.0)),
        "D_skip": 1.0 + nrm(ks[19], (DEPTH, M_HEADS), 0.02),
        "m_norm_w": 1.0 + nrm(ks[20], (DEPTH, D_INNER), 0.02),
        "w_m_out": nrm(ks[21], (DEPTH, D_INNER, D_MODEL), D_INNER ** -0.5),
        "w_o": nrm(ks[22], (DEPTH, D_MODEL, D_MODEL), D_MODEL ** -0.5),
        "norm2_w": 1.0 + nrm(ks[23], (DEPTH, D_MODEL), 0.02),
        "peer_w_q": nrm(ks[24], (DEPTH, D_MODEL, PEER_HEADS * D_KEY), D_MODEL ** -0.5),
        "peer_keys": nrm(ks[25], (DEPTH, PEER_HEADS, 2, N_KEYS, D_KEY // 2), (D_KEY // 2) ** -0.5),
        "peer_u": nrm(ks[26], (DEPTH, N_EXPERTS, D_MODEL), D_MODEL ** -0.5),
        "peer_v": nrm(ks[27], (DEPTH, N_EXPERTS, D_MODEL), (PEER_HEADS * PEER_TOPK) ** -0.5),
        "final_norm_w": 1.0 + nrm(ks[28], (D_MODEL,), 0.02),
    }


def reference(x_prompt, x_sample, cache_conf, cache_mconv, state_ssm, meta_tokens,
              norm1_w, w_in, b_in, conf_dw_w, conf_dw_b, conf_ln_g, conf_ln_b, w_conf_out, b_conf_out,
              m_conv_w, m_conv_b, dt_bias, A_log, D_skip, m_norm_w, w_m_out, w_o, norm2_w,
              peer_w_q, peer_keys, peer_u, peer_v, final_norm_w):
    bp, sp = x_prompt.shape[:2]
    xp = jnp.concatenate([jnp.broadcast_to(meta_tokens[None].astype(x_prompt.dtype), (bp, N_META, D_MODEL)),
                          x_prompt], axis=1)
    xs = x_sample
    prompt_segments = [(N_META, N_META), (sp, SSD_CHUNK)]
    sample_segments = [(xs.shape[1], xs.shape[1])]
    conf_p, mconv_p, ssm_p, conf_s, mconv_s, ssm_s = [], [], [], [], [], []
    for l in range(DEPTH):
        p = (norm1_w[l], w_in[l], b_in[l], conf_dw_w[l], conf_dw_b[l], conf_ln_g[l], conf_ln_b[l],
             w_conf_out[l], b_conf_out[l], m_conv_w[l], m_conv_b[l], dt_bias[l], A_log[l], D_skip[l],
             m_norm_w[l], w_m_out[l], w_o[l], norm2_w[l], peer_w_q[l], peer_keys[l], peer_u[l], peer_v[l])
        zc = jnp.zeros((bp, CONF_W - 1, C_CONV), xp.dtype)
        zm = jnp.zeros((bp, M_CONV_W - 1, CONV_DIM), xp.dtype)
        zh = jnp.zeros((bp, M_HEADS, M_HEADDIM, M_STATE), xp.dtype)
        xp, c1, m1, h1 = trunk_layer(xp, zc, zm, zh, prompt_segments, p)
        xs, c2, m2, h2 = trunk_layer(xs, cache_conf[l], cache_mconv[l], state_ssm[l], sample_segments, p)
        conf_p.append(c1); mconv_p.append(m1); ssm_p.append(h1)
        conf_s.append(c2); mconv_s.append(m2); ssm_s.append(h2)
    y_prompt = rmsnorm(xp, final_norm_w)[:, N_META:]
    y_sample = rmsnorm(xs, final_norm_w)
    return (y_prompt, y_sample,
            jnp.stack(conf_p), jnp.stack(mconv_p), jnp.stack(ssm_p),
            jnp.stack(conf_s), jnp.stack(mconv_s), jnp.stack(ssm_s))
```

```python
import functools
import math

import jax
import jax.numpy as jnp
from jax import lax
from jax.experimental import pallas as pl
from jax.experimental.pallas import tpu as pltpu

F32 = jnp.float32
BF16 = jnp.bfloat16

EPS = 1e-6
N_META = 16
CONF_W = 31
M_CONV_W = 4
M_HEADDIM = 64
M_GROUPS = 8
M_STATE = 128
PEER_TOPK = 16
LANES = 128
SUBLANES = 8
CONF_HIST = 32
MCONV_HIST = 8
META_ROWS = 128
ROW_ALIGN = 512
VMEM_LIMIT = 56 * 1024 * 1024


def _cparams(sem):
    return pltpu.CompilerParams(dimension_semantics=sem, vmem_limit_bytes=VMEM_LIMIT)


def _split3(a):
    p1 = a.astype(BF16)
    r1 = a - p1.astype(F32)
    p2 = r1.astype(BF16)
    r2 = r1 - p2.astype(F32)
    p3 = r2.astype(BF16)
    return p1, p2, p3


def _dot(a, b):
    return jnp.dot(a, b, preferred_element_type=F32)


def _dot_nt(a, b):
    return lax.dot_general(a, b, (((1,), (1,)), ((), ())), preferred_element_type=F32)


def _dot_tn(a, b):
    return lax.dot_general(a, b, (((0,), (0,)), ((), ())), preferred_element_type=F32)


def _sel_right(m01, a):
    p1, p2, p3 = _split3(a)
    return _dot(m01, p1) + _dot(m01, p2) + _dot(m01, p3)


def _sel_left(a, m01):
    p1, p2, p3 = _split3(a)
    return _dot(p1, m01) + _dot(p2, m01) + _dot(p3, m01)


def _transpose_id(a, n):
    eye = (lax.broadcasted_iota(jnp.int32, (n, n), 0)
           == lax.broadcasted_iota(jnp.int32, (n, n), 1)).astype(BF16)
    p1, p2, p3 = _split3(a)
    return _dot_nt(eye, p1) + _dot_nt(eye, p2) + _dot_nt(eye, p3)


def _sigmoid(x):
    return 1.0 / (1.0 + jnp.exp(-x))


def _lin_kernel(x_ref, nw_ref, w_ref, b_ref, o_ref, xn_ref):
    @pl.when(pl.program_id(1) == 0)
    def _():
        x = x_ref[...]
        ms = jnp.mean(x * x, axis=-1, keepdims=True)
        xn_ref[...] = (x * lax.rsqrt(ms + EPS) * nw_ref[...]).astype(BF16)

    o_ref[...] = _dot(xn_ref[...], w_ref[...]) + b_ref[...]


def _norm_linear(x, nw, w, b, tm, tn):
    n, d = x.shape
    nc = w.shape[1]
    return pl.pallas_call(
        _lin_kernel,
        grid=(n // tm, nc // tn),
        in_specs=[
            pl.BlockSpec((tm, d), lambda i, j: (i, 0)),
            pl.BlockSpec((1, d), lambda i, j: (0, 0)),
            pl.BlockSpec((d, tn), lambda i, j: (0, j)),
            pl.BlockSpec((1, tn), lambda i, j: (0, j)),
        ],
        out_specs=pl.BlockSpec((tm, tn), lambda i, j: (i, j)),
        out_shape=jax.ShapeDtypeStruct((n, nc), F32),
        scratch_shapes=[pltpu.VMEM((tm, d), BF16)],
        compiler_params=_cparams(("parallel", "arbitrary")),
        name="norm_linear",
    )(x, nw, w, b)


def _conf_kernel(ga_ref, gb_ref, cache_ref, dww_ref, dwb_ref, lng_ref, lnb_ref, wout_ref, bout_ref,
                 ya_ref, hist_ref, win_ref, *, nb, ls, lv, rb):
    @pl.when(pl.program_id(1) == 0)
    def _():
        win_ref[:, 0:CONF_HIST, :] = cache_ref[...]

    u = ga_ref[...] * _sigmoid(gb_ref[...])
    for j in range(nb):
        win_ref[j, CONF_HIST:CONF_HIST + ls, :] = u[j * ls:(j + 1) * ls, :]

    off = CONF_HIST - (CONF_W - 1)
    blocks = []
    for j in range(nb):
        for r0 in range(0, ls, rb):
            acc = jnp.broadcast_to(dwb_ref[...], (rb, dwb_ref.shape[1]))
            for k in range(CONF_W):
                acc = acc + win_ref[j, r0 + k + off:r0 + k + off + rb, :] * dww_ref[k:k + 1, :]
            blocks.append(acc)
    c = blocks[0] if len(blocks) == 1 else jnp.concatenate(blocks, axis=0)

    mu = jnp.mean(c, axis=-1, keepdims=True)
    xc = c - mu
    y = xc * lax.rsqrt(jnp.mean(xc * xc, axis=-1, keepdims=True) + EPS)
    y = y * lng_ref[...] + lnb_ref[...]
    y = y * _sigmoid(y)
    ya_ref[...] = _dot(y.astype(BF16), wout_ref[...]) + bout_ref[...]

    for j in range(nb):
        tail = win_ref[j, lv:lv + CONF_HIST, :]
        win_ref[j, 0:CONF_HIST, :] = tail
    hist_ref[...] = win_ref[:, 0:CONF_HIST, :]


def _conf_branch(proj, cache, p, *, row_off, nseq, nb, ls, lv, n_tiles, cache_bcast):
    c = p["dww"].shape[1]
    d = p["wout"].shape[1]
    r = nb * ls
    rb = min(ls, 16)
    bsteps = nseq // nb
    blk0 = row_off // r
    row_map = lambda b, i: (blk0 + b * n_tiles + i, 0)
    cache_map = (lambda b, i: (0, 0, 0)) if cache_bcast else (lambda b, i: (b, 0, 0))
    const2 = lambda b, i: (0, 0)
    kern = functools.partial(_conf_kernel, nb=nb, ls=ls, lv=lv, rb=rb)
    ya, hist = pl.pallas_call(
        kern,
        grid=(bsteps, n_tiles),
        in_specs=[
            pl.BlockSpec((r, c), row_map),
            pl.BlockSpec((r, c), lambda b, i: (blk0 + b * n_tiles + i, 1)),
            pl.BlockSpec((nb, CONF_HIST, c), cache_map),
            pl.BlockSpec((CONF_HIST, c), const2),
            pl.BlockSpec((1, c), const2),
            pl.BlockSpec((1, c), const2),
            pl.BlockSpec((1, c), const2),
            pl.BlockSpec((c, d), const2),
            pl.BlockSpec((1, d), const2),
        ],
        out_specs=[
            pl.BlockSpec((r, d), lambda b, i: (b * n_tiles + i, 0)),
            pl.BlockSpec((nb, CONF_HIST, c), lambda b, i: (b, 0, 0)),
        ],
        out_shape=[
            jax.ShapeDtypeStruct((bsteps * n_tiles * r, d), F32),
            jax.ShapeDtypeStruct((nseq, CONF_HIST, c), F32),
        ],
        scratch_shapes=[pltpu.VMEM((nb, CONF_HIST + ls, c), F32)],
        compiler_params=_cparams(("parallel", "arbitrary")),
        name="conformer_branch",
    )(proj, proj, cache, p["dww"], p["dwb"], p["lng"], p["lnb"], p["wout"], p["bout"])
    return ya, hist


def _ssd_kernel(xbc_ref, z_ref, dt_ref, mc_ref, h0_ref, cw_ref, cb_ref, dtb_ref, alog_ref, dsk_ref,
                nw_ref, wout_ref, yb_ref, hout_ref, mcout_ref, cwin_scr, y_scr,
                *, nb, ls, lv, nheads, hdim, ngroups, nstate):
    r = nb * ls
    di = nheads * hdim
    gn = ngroups * nstate
    hpg = nheads // ngroups
    gw = hpg * hdim
    ls_shift = int(math.log2(ls))
    i = pl.program_id(1)

    @pl.when(i == 0)
    def _():
        hout_ref[...] = h0_ref[...]
        cwin_scr[:, 0:MCONV_HIST, :] = mc_ref[...]

    xbc = xbc_ref[...]
    for j in range(nb):
        cwin_scr[j, MCONV_HIST:MCONV_HIST + ls, :] = xbc[j * ls:(j + 1) * ls, :]
    off = MCONV_HIST - (M_CONV_W - 1)
    parts = []
    for j in range(nb):
        acc = jnp.broadcast_to(cb_ref[...], (ls, cb_ref.shape[1]))
        for k in range(M_CONV_W):
            acc = acc + cwin_scr[j, off + k:off + k + ls, :] * cw_ref[k:k + 1, :]
        parts.append(acc)
    xc = parts[0] if nb == 1 else jnp.concatenate(parts, axis=0)
    xc = xc * _sigmoid(xc)
    for j in range(nb):
        tail = cwin_scr[j, lv:lv + MCONV_HIST, :]
        cwin_scr[j, 0:MCONV_HIST, :] = tail
    mcout_ref[...] = cwin_scr[:, 0:MCONV_HIST, :]

    xs = xc[:, :di]
    bm = xc[:, di:di + gn].astype(BF16)
    cm = xc[:, di + gn:].astype(BF16)

    row1 = lax.broadcasted_iota(jnp.int32, (r, 1), 0)
    valid = (row1 & (ls - 1)) < lv
    dt = jax.nn.softplus(dt_ref[...] + dtb_ref[...])
    dt = jnp.where(valid, dt, 0.0)
    a = dt * (-jnp.exp(alog_ref[...]))

    ri = lax.broadcasted_iota(jnp.int32, (r, r), 0)
    ci = lax.broadcasted_iota(jnp.int32, (r, r), 1)
    same = (ri >> ls_shift) == (ci >> ls_shift)
    causal = same & (ci <= ri)
    acs = _sel_right(causal.astype(BF16), a)
    tot = _sel_right(same.astype(BF16), a)
    acs_t = _transpose_id(acs, LANES)
    etot_t = jnp.exp(_transpose_id(tot, LANES))

    expand = (lax.broadcasted_iota(jnp.int32, (LANES, di), 0)
              == (lax.broadcasted_iota(jnp.int32, (LANES, di), 1) >> int(math.log2(hdim)))).astype(BF16)
    dt_rep = _sel_left(dt, expand)
    eacs_rep = _sel_left(jnp.exp(acs), expand)
    wend_rep = _sel_left(jnp.exp(tot - acs) * dt, expand)
    dx = (xs * dt_rep).astype(BF16)
    wdx = xs * wend_rep

    lane = lax.broadcasted_iota(jnp.int32, (r, LANES), 1)
    heads_per_blk = LANES // hdim
    for g in range(ngroups):
        bg = bm[:, g * nstate:(g + 1) * nstate]
        cg = cm[:, g * nstate:(g + 1) * nstate]
        cb = _dot_nt(cg, bg)
        for blk in range(hpg // heads_per_blk):
            col0 = g * gw + blk * LANES
            dxb = dx[:, col0:col0 + LANES]
            yacc = None
            for q in range(heads_per_blk):
                h = g * hpg + blk * heads_per_blk + q
                seg = acs[:, h:h + 1] - acs_t[h:h + 1, :]
                dec = jnp.exp(jnp.where(causal, seg, -jnp.inf))
                m = (cb * dec).astype(BF16)
                hm = (lane >= q * hdim) & (lane < (q + 1) * hdim)
                t = _dot(m, jnp.where(hm, dxb, jnp.zeros_like(dxb)))
                yacc = t if yacc is None else yacc + t
            y_scr[:, col0:col0 + LANES] = yacc

    col_l = lax.broadcasted_iota(jnp.int32, (r, LANES), 0)

    def seq_body(j, carry):
        if nb == 1:
            wdx_j = wdx
            mrow = None
        else:
            mrow = (row1 >> ls_shift) == j
            wdx_j = jnp.where(mrow, wdx, 0.0)
        wdx_j = wdx_j.astype(BF16)
        selc = (col_l == (j + 1) * ls - 1).astype(BF16)
        el = _sel_left(etot_t, selc)
        for g in range(ngroups):
            bg = bm[:, g * nstate:(g + 1) * nstate]
            cg = cm[:, g * nstate:(g + 1) * nstate]
            hg = jnp.concatenate([hout_ref[j, g * hpg + q] for q in range(hpg)], axis=0)
            yoff = _dot_nt(cg, hg.astype(BF16)) * eacs_rep[:, g * gw:(g + 1) * gw]
            if mrow is not None:
                yoff = jnp.where(mrow, yoff, 0.0)
            y_scr[:, g * gw:(g + 1) * gw] += yoff
            st = _dot_tn(wdx_j[:, g * gw:(g + 1) * gw], bg)
            for q in range(hpg):
                h = g * hpg + q
                scale = jnp.broadcast_to(el[h:h + 1, :], (hdim, nstate))
                hout_ref[j, h] = scale * hout_ref[j, h] + st[q * hdim:(q + 1) * hdim, :]
        return carry

    if nb == 1:
        seq_body(0, 0)
    else:
        lax.fori_loop(0, nb, seq_body, 0)

    y = y_scr[...] + dsk_ref[...] * xs
    z = z_ref[...]
    y = y * (z * _sigmoid(z))
    outs = []
    gsz = di // M_GROUPS
    for g in range(M_GROUPS):
        blk = y[:, g * gsz:(g + 1) * gsz]
        outs.append(blk * lax.rsqrt(jnp.mean(blk * blk, axis=-1, keepdims=True) + EPS))
    yn = jnp.concatenate(outs, axis=1) * nw_ref[...]
    yb_ref[...] = _dot(yn.astype(BF16), wout_ref[...])


def _ssd_branch(proj, mcache, h0, p, cols, *, row_off, nseq, nb, ls, lv, n_chunks, state_bcast):
    nheads, hdim, nstate = h0.shape[1], h0.shape[2], h0.shape[3]
    di = nheads * hdim
    cd = p["mcw"].shape[1]
    d = p["wmout"].shape[1]
    r = nb * ls
    bsteps = nseq // nb
    blk0 = row_off // r
    const2 = lambda b, i: (0, 0)
    st_map = (lambda b, i: (0, 0, 0, 0)) if state_bcast else (lambda b, i: (b, 0, 0, 0))
    mc_map = (lambda b, i: (0, 0, 0)) if state_bcast else (lambda b, i: (b, 0, 0))
    kern = functools.partial(_ssd_kernel, nb=nb, ls=ls, lv=lv, nheads=nheads, hdim=hdim,
                             ngroups=M_GROUPS, nstate=nstate)
    yb, hout, mcout = pl.pallas_call(
        kern,
        grid=(bsteps, n_chunks),
        in_specs=[
            pl.BlockSpec((r, cd), lambda b, i: (blk0 + b * n_chunks + i, cols["xbc"] // cd)),
            pl.BlockSpec((r, di), lambda b, i: (blk0 + b * n_chunks + i, cols["z"] // di)),
            pl.BlockSpec((r, LANES), lambda b, i: (blk0 + b * n_chunks + i, cols["dt"] // LANES)),
            pl.BlockSpec((nb, MCONV_HIST, cd), mc_map),
            pl.BlockSpec((nb, nheads, hdim, nstate), st_map),
            pl.BlockSpec((MCONV_HIST, cd), const2),
            pl.BlockSpec((1, cd), const2),
            pl.BlockSpec((1, LANES), const2),
            pl.BlockSpec((1, LANES), const2),
            pl.BlockSpec((1, di), const2),
            pl.BlockSpec((1, di), const2),
            pl.BlockSpec((di, d), const2),
        ],
        out_specs=[
            pl.BlockSpec((r, d), lambda b, i: (b * n_chunks + i, 0)),
            pl.BlockSpec((nb, nheads, hdim, nstate), lambda b, i: (b, 0, 0, 0)),
            pl.BlockSpec((nb, MCONV_HIST, cd), lambda b, i: (b, 0, 0)),
        ],
        out_shape=[
            jax.ShapeDtypeStruct((bsteps * n_chunks * r, d), F32),
            jax.ShapeDtypeStruct((nseq, nheads, hdim, nstate), F32),
            jax.ShapeDtypeStruct((nseq, MCONV_HIST, cd), F32),
        ],
        scratch_shapes=[
            pltpu.VMEM((nb, MCONV_HIST + ls, cd), F32),
            pltpu.VMEM((r, di), F32),
        ],
        compiler_params=_cparams(("parallel", "arbitrary")),
        name="ssd_branch",
    )(proj, proj, proj, mcache, h0, p["mcw"], p["mcb"], p["dtb"], p["alog"], p["dskip"],
      p["mnw"], p["wmout"])
    return yb, hout, mcout


def _topk_rows(s, k):
    n = s.shape[0]
    rows = lax.broadcasted_iota(jnp.int32, s.shape, 0)
    rank = jnp.full(s.shape, float(k), F32)
    vals = []
    for j in range(k):
        m = jnp.max(s, axis=0, keepdims=True)
        idx = jnp.min(jnp.where(s == m, rows, n), axis=0, keepdims=True)
        hit = rows == idx
        rank = jnp.where(hit, float(j), rank)
        s = jnp.where(hit, -jnp.inf, s)
        vals.append(m)
    return vals, rank


def _merge_route_kernel(x_ref, ya_ref, yb_ref, g_ref, wo_ref, n2_ref, wq_ref, keys_ref,
                        x1_ref, xn_ref, rt_ref, *, nheads, nkeys, dhalf):
    d = x_ref.shape[1]
    g = _sigmoid(g_ref[...])
    mix = g[:, :d] * ya_ref[...] + g[:, d:] * yb_ref[...]
    x1 = x_ref[...] + _dot(mix.astype(BF16), wo_ref[...])
    x1_ref[...] = x1
    ms = jnp.mean(x1 * x1, axis=-1, keepdims=True)
    xn = (x1 * lax.rsqrt(ms + EPS) * n2_ref[...]).astype(BF16)
    xn_ref[...] = xn
    q = _dot(xn, wq_ref[...]).astype(BF16)

    k = PEER_TOPK
    for h in range(nheads):
        s1 = _dot_nt(keys_ref[h, 0], q[:, (2 * h) * dhalf:(2 * h + 1) * dhalf])
        s2 = _dot_nt(keys_ref[h, 1], q[:, (2 * h + 1) * dhalf:(2 * h + 2) * dhalf])
        v1, rank1 = _topk_rows(s1, k)
        v2, rank2 = _topk_rows(s2, k)
        v2m = jnp.concatenate(v2, axis=0)
        cand = jnp.concatenate([v1[a] + v2m for a in range(k)], axis=0)
        rows = lax.broadcasted_iota(jnp.int32, cand.shape, 0)
        c = cand
        sel = jnp.zeros(cand.shape, jnp.bool_)
        for _ in range(k):
            m = jnp.max(c, axis=0, keepdims=True)
            idx = jnp.min(jnp.where(c == m, rows, k * k), axis=0, keepdims=True)
            hit = rows == idx
            sel = sel | hit
            c = jnp.where(hit, -jnp.inf, c)
        cmax = v1[0] + v2[0]
        zsum = jnp.sum(jnp.where(sel, jnp.exp(cand - cmax), 0.0), axis=0, keepdims=True)
        self32 = sel.astype(F32)
        cn = jnp.zeros(s1.shape, F32)
        for a in range(k):
            cnt_a = jnp.sum(self32[a * k:(a + 1) * k], axis=0, keepdims=True)
            cn = cn + jnp.where(rank1 == float(a), cnt_a, 0.0)
        rt_ref[h, 0] = cn
        rt_ref[h, 1] = jnp.exp(s1 - v1[0]) / zsum
        rt_ref[h, 2] = rank2
        rt_ref[h, 3] = jnp.exp(s2 - v2[0])


def _merge_route(x, ya, yb, proj, p, cols, tm):
    n, d = x.shape
    nheads, _, nkeys, dhalf = p["keys"].shape
    dq = p["wq"].shape[1]
    const2 = lambda i: (0, 0)
    kern = functools.partial(_merge_route_kernel, nheads=nheads, nkeys=nkeys, dhalf=dhalf)
    return pl.pallas_call(
        kern,
        grid=(n // tm,),
        in_specs=[
            pl.BlockSpec((tm, d), lambda i: (i, 0)),
            pl.BlockSpec((tm, d), lambda i: (i, 0)),
            pl.BlockSpec((tm, d), lambda i: (i, 0)),
            pl.BlockSpec((tm, 2 * d), lambda i: (i, cols["gate"] // (2 * d))),
            pl.BlockSpec((d, d), const2),
            pl.BlockSpec((1, d), const2),
            pl.BlockSpec((d, dq), const2),
            pl.BlockSpec((nheads, 2, nkeys, dhalf), lambda i: (0, 0, 0, 0)),
        ],
        out_specs=[
            pl.BlockSpec((tm, d), lambda i: (i, 0)),
            pl.BlockSpec((tm, d), lambda i: (i, 0)),
            pl.BlockSpec((nheads, 4, nkeys, tm), lambda i: (0, 0, 0, i)),
        ],
        out_shape=[
            jax.ShapeDtypeStruct((n, d), F32),
            jax.ShapeDtypeStruct((n, d), BF16),
            jax.ShapeDtypeStruct((nheads, 4, nkeys, n), F32),
        ],
        compiler_params=_cparams(("parallel",)),
        name="merge_route",
    )(x, ya, yb, proj, p["wo"], p["n2"], p["wq"], p["keys"])


def _peer_kernel(xn_ref, rt_ref, u_ref, vt_ref, x1_ref, fw_ref, o_ref, acc_ref,
                 *, nheads, nkeys, final_norm):
    e = pl.program_id(1)
    eb = u_ref.shape[0]
    na = eb // nkeys

    @pl.when(e == 0)
    def _():
        acc_ref[...] = jnp.zeros_like(acc_ref)

    ht = _dot_nt(u_ref[...], xn_ref[...])
    a0 = pl.multiple_of(e * na, na)
    slabs = [(rt_ref[h, 0, pl.ds(a0, na), :], rt_ref[h, 1, pl.ds(a0, na), :]) for h in range(nheads)]
    ws = []
    for al in range(na):
        gate = None
        for h in range(nheads):
            cn = slabs[h][0][al:al + 1, :]
            e1 = slabs[h][1][al:al + 1, :]
            t = jnp.where(rt_ref[h, 2] < cn, rt_ref[h, 3], 0.0) * e1
            gate = t if gate is None else gate + t
        hv = ht[al * nkeys:(al + 1) * nkeys, :]
        act = 0.5 * hv * (1.0 + lax.erf(hv * (1.0 / math.sqrt(2.0))))
        ws.append((gate * act).astype(BF16))
    wt = ws[0] if na == 1 else jnp.concatenate(ws, axis=0)
    acc_ref[...] += _dot(vt_ref[...], wt)

    @pl.when(e == pl.num_programs(1) - 1)
    def _():
        x2 = x1_ref[...] + acc_ref[...].T
        if final_norm:
            ms = jnp.mean(x2 * x2, axis=-1, keepdims=True)
            x2 = x2 * lax.rsqrt(ms + EPS) * fw_ref[...]
        o_ref[...] = x2


def _peer(xn, rt, u_bf, vt_bf, x1, fw, tt, eb, final_norm):
    n, d = x1.shape
    ne = u_bf.shape[0]
    nheads, _, nkeys, _ = rt.shape
    kern = functools.partial(_peer_kernel, nheads=nheads, nkeys=nkeys, final_norm=final_norm)
    return pl.pallas_call(
        kern,
        grid=(n // tt, ne // eb),
        in_specs=[
            pl.BlockSpec((tt, d), lambda i, e: (i, 0)),
            pl.BlockSpec((nheads, 4, nkeys, tt), lambda i, e: (0, 0, 0, i)),
            pl.BlockSpec((eb, d), lambda i, e: (e, 0)),
            pl.BlockSpec((d, eb), lambda i, e: (0, e)),
            pl.BlockSpec((tt, d), lambda i, e: (i, 0)),
            pl.BlockSpec((1, d), lambda i, e: (0, 0)),
        ],
        out_specs=pl.BlockSpec((tt, d), lambda i, e: (i, 0)),
        out_shape=jax.ShapeDtypeStruct((n, d), F32),
        scratch_shapes=[pltpu.VMEM((d, tt), F32)],
        compiler_params=_cparams(("parallel", "arbitrary")),
        name="peer_experts",
    )(xn, rt, u_bf, vt_bf, x1, fw)


def _pad_rows(a, n):
    return jnp.pad(a, ((0, n - a.shape[0]),) + ((0, 0),) * (a.ndim - 1))


def kernel(x_prompt, x_sample, cache_conf, cache_mconv, state_ssm, meta_tokens, norm1_w, w_in, b_in, conf_dw_w, conf_dw_b, conf_ln_g, conf_ln_b, w_conf_out, b_conf_out, m_conv_w, m_conv_b, dt_bias, A_log, D_skip, m_norm_w, w_m_out, w_o, norm2_w, peer_w_q, peer_keys, peer_u, peer_v, final_norm_w):
    bp, sp, d = x_prompt.shape
    db, ds, _ = x_sample.shape
    depth = norm1_w.shape[0]
    c_conv = conf_dw_w.shape[2]
    nheads = dt_bias.shape[1]
    di = nheads * M_HEADDIM
    gn = M_GROUPS * M_STATE
    cd = di + 2 * gn
    n_p, n_s = bp * sp, db * ds
    n_all = -(-(n_p + n_s + META_ROWS) // ROW_ALIGN) * ROW_ALIGN
    off_s, off_m = n_p, n_p + n_s

    cols = {"glu_a": 0, "glu_b": c_conv, "z": 2 * c_conv, "xbc": 2 * c_conv + di}
    assert cols["z"] % di == 0 and cols["xbc"] % cd == 0
    cols["gate"] = cols["xbc"] + cd
    assert cols["gate"] % (2 * d) == 0
    cols["dt"] = cols["gate"] + 2 * d
    ncol = cols["dt"] + LANES
    src_dt = 2 * c_conv + di + cd
    src_gate = src_dt + nheads
    tn = ncol // 9 if (ncol % 9 == 0 and (ncol // 9) % LANES == 0) else LANES

    seq_nb = 8
    prompt_ls = 128
    assert sp % prompt_ls == 0 and db % seq_nb == 0 and n_p % LANES == 0 and n_s % LANES == 0

    x_all = jnp.concatenate([
        x_prompt.reshape(n_p, d), x_sample.reshape(n_s, d), meta_tokens.astype(F32),
        jnp.zeros((n_all - n_p - n_s - N_META, d), F32)], axis=0)

    outs = {k: [] for k in ("conf_p", "mconv_p", "ssm_p", "conf_s", "mconv_s", "ssm_s")}
    for l in range(depth):
        wl, bl = w_in[l], b_in[l]
        w_re = jnp.concatenate([wl[:, :src_dt], wl[:, src_gate:], wl[:, src_dt:src_gate],
                                jnp.zeros((d, LANES - nheads), F32)], axis=1).astype(BF16)
        b_re = jnp.concatenate([bl[:src_dt], bl[src_gate:], bl[src_dt:src_gate],
                                jnp.zeros((LANES - nheads,), F32)])[None, :]
        proj = _norm_linear(x_all, norm1_w[l][None, :], w_re, b_re, ROW_ALIGN, tn)

        pc = {
            "dww": _pad_rows(conf_dw_w[l], CONF_HIST), "dwb": conf_dw_b[l][None, :],
            "lng": conf_ln_g[l][None, :], "lnb": conf_ln_b[l][None, :],
            "wout": w_conf_out[l].astype(BF16), "bout": b_conf_out[l][None, :],
        }
        ps = {
            "mcw": _pad_rows(m_conv_w[l], MCONV_HIST), "mcb": m_conv_b[l][None, :],
            "dtb": jnp.pad(dt_bias[l], (0, LANES - nheads))[None, :],
            "alog": jnp.pad(A_log[l], (0, LANES - nheads))[None, :],
            "dskip": jnp.repeat(D_skip[l], M_HEADDIM)[None, :],
            "mnw": m_norm_w[l][None, :], "wmout": w_m_out[l].astype(BF16),
        }

        ya_m, cf_m = _conf_branch(proj, jnp.zeros((1, CONF_HIST, c_conv), F32), pc, row_off=off_m,
                                  nseq=1, nb=1, ls=META_ROWS, lv=N_META, n_tiles=1, cache_bcast=False)
        yb_m, h_m, mc_m = _ssd_branch(proj, jnp.zeros((1, MCONV_HIST, cd), F32),
                                      jnp.zeros((1, nheads, M_HEADDIM, M_STATE), F32), ps, cols,
                                      row_off=off_m, nseq=1, nb=1, ls=META_ROWS, lv=N_META,
                                      n_chunks=1, state_bcast=False)
        ya_p, cf_p = _conf_branch(proj, cf_m, pc, row_off=0, nseq=bp, nb=1, ls=prompt_ls, lv=prompt_ls,
                                  n_tiles=sp // prompt_ls, cache_bcast=True)
        yb_p, h_p, mc_p = _ssd_branch(proj, mc_m, h_m, ps, cols, row_off=0, nseq=bp, nb=1,
                                      ls=prompt_ls, lv=prompt_ls, n_chunks=sp // prompt_ls,
                                      state_bcast=True)
        cf_in = jnp.pad(cache_conf[l], ((0, 0), (CONF_HIST - (CONF_W - 1), 0), (0, 0)))
        mc_in = jnp.pad(cache_mconv[l], ((0, 0), (MCONV_HIST - (M_CONV_W - 1), 0), (0, 0)))
        ya_s, cf_s = _conf_branch(proj, cf_in, pc, row_off=off_s, nseq=db, nb=seq_nb, ls=ds, lv=ds,
                                  n_tiles=1, cache_bcast=False)
        yb_s, h_s, mc_s = _ssd_branch(proj, mc_in, state_ssm[l], ps, cols, row_off=off_s, nseq=db,
                                      nb=seq_nb, ls=ds, lv=ds, n_chunks=1, state_bcast=False)

        tail = jnp.zeros((n_all - off_m - META_ROWS, d), F32)
        ya = jnp.concatenate([ya_p, ya_s, ya_m, tail], axis=0)
        yb = jnp.concatenate([yb_p, yb_s, yb_m, tail], axis=0)

        pm = {"wo": w_o[l].astype(BF16), "n2": norm2_w[l][None, :], "wq": peer_w_q[l].astype(BF16),
              "keys": peer_keys[l].astype(BF16)}
        x1, xn2, rt = _merge_route(x_all, ya, yb, proj, pm, cols, 256)
        x_all = _peer(xn2, rt, peer_u[l].astype(BF16), peer_v[l].T.astype(BF16), x1,
                      final_norm_w[None, :], 256, 8 * peer_keys.shape[3], l == depth - 1)

        outs["conf_p"].append(cf_p[:, CONF_HIST - (CONF_W - 1):])
        outs["mconv_p"].append(mc_p[:, MCONV_HIST - (M_CONV_W - 1):])
        outs["ssm_p"].append(h_p)
        outs["conf_s"].append(cf_s[:, CONF_HIST - (CONF_W - 1):])
        outs["mconv_s"].append(mc_s[:, MCONV_HIST - (M_CONV_W - 1):])
        outs["ssm_s"].append(h_s)

    y_prompt = x_all[:n_p].reshape(bp, sp, d)
    y_sample = x_all[off_s:off_s + n_s].reshape(db, ds, d)
    return (y_prompt, y_sample,
            jnp.stack(outs["conf_p"]), jnp.stack(outs["mconv_p"]), jnp.stack(outs["ssm_p"]),
            jnp.stack(outs["conf_s"]), jnp.stack(outs["mconv_s"]), jnp.stack(outs["ssm_s"]))
```

```python
import functools
import math

import jax
import jax.numpy as jnp
from jax import lax
from jax.experimental import pallas as pl
from jax.experimental.pallas import tpu as pltpu

F32 = jnp.float32
BF16 = jnp.bfloat16

EPS = 1e-6
N_META = 16
CONF_W = 31
M_CONV_W = 4
M_HEADDIM = 64
M_GROUPS = 8
M_STATE = 128
PEER_TOPK = 16
LANES = 128
SUBLANES = 8
BF16_ROWS = 16
CONF_HIST = 32
MCONV_HIST = 8
META_ROWS = 128
ROW_ALIGN = 512
LIN_ROWS = 1280
VMEM_LIMIT = 56 * 1024 * 1024


def _cparams(sem):
    return pltpu.CompilerParams(dimension_semantics=sem, vmem_limit_bytes=VMEM_LIMIT)


def _split3(a):
    p1 = a.astype(BF16)
    r1 = a - p1.astype(F32)
    p2 = r1.astype(BF16)
    r2 = r1 - p2.astype(F32)
    p3 = r2.astype(BF16)
    return p1, p2, p3


def _dot(a, b):
    return jnp.dot(a, b, preferred_element_type=F32)


def _dot_nt(a, b):
    return lax.dot_general(a, b, (((1,), (1,)), ((), ())), preferred_element_type=F32)


def _dot_tn(a, b):
    return lax.dot_general(a, b, (((0,), (0,)), ((), ())), preferred_element_type=F32)


def _sel_right(m01, a):
    p1, p2, p3 = _split3(a)
    return _dot(m01, p1) + _dot(m01, p2) + _dot(m01, p3)


def _sel_left(a, m01):
    p1, p2, p3 = _split3(a)
    return _dot(p1, m01) + _dot(p2, m01) + _dot(p3, m01)


def _transpose_id(a, n):
    eye = (lax.broadcasted_iota(jnp.int32, (n, n), 0)
           == lax.broadcasted_iota(jnp.int32, (n, n), 1)).astype(BF16)
    p1, p2, p3 = _split3(a)
    return _dot_nt(eye, p1) + _dot_nt(eye, p2) + _dot_nt(eye, p3)


def _sigmoid(x):
    return 1.0 / (1.0 + jnp.exp(-x))


def _lin_kernel(x_ref, nw_ref, w_ref, b_ref, o_ref, xn_ref):
    @pl.when(pl.program_id(1) == 0)
    def _():
        x = x_ref[...]
        ms = jnp.mean(x * x, axis=-1, keepdims=True)
        xn_ref[...] = (x * lax.rsqrt(ms + EPS) * nw_ref[...]).astype(BF16)

    o_ref[...] = _dot(xn_ref[...], w_ref[...]) + b_ref[...]


def _norm_linear(x, nw, w, b, tm, tn):
    n, d = x.shape
    nc = w.shape[1]
    return pl.pallas_call(
        _lin_kernel,
        grid=(n // tm, nc // tn),
        in_specs=[
            pl.BlockSpec((tm, d), lambda i, j: (i, 0)),
            pl.BlockSpec((1, d), lambda i, j: (0, 0)),
            pl.BlockSpec((d, tn), lambda i, j: (0, j)),
            pl.BlockSpec((1, tn), lambda i, j: (0, j)),
        ],
        out_specs=pl.BlockSpec((tm, tn), lambda i, j: (i, j)),
        out_shape=jax.ShapeDtypeStruct((n, nc), F32),
        scratch_shapes=[pltpu.VMEM((tm, d), BF16)],
        compiler_params=_cparams(("parallel", "arbitrary")),
        name="norm_linear",
    )(x, nw, w, b)


def _conf_kernel(ga_ref, gb_ref, cache_ref, dww_ref, dwb_ref, lng_ref, lnb_ref, wout_ref, bout_ref,
                 ya_ref, hist_ref, win_ref, c_scr, view_scr, *, nb, ls, lv):
    @pl.when(pl.program_id(1) == 0)
    def _():
        win_ref[:, 0:CONF_HIST, :] = cache_ref[...]

    u = ga_ref[...] * _sigmoid(gb_ref[...])
    for j in range(nb):
        win_ref[j, CONF_HIST:CONF_HIST + ls, :] = u[j * ls:(j + 1) * ls, :]

    off = CONF_HIST - (CONF_W - 1)
    for j in range(nb):
        for cb in range(dwb_ref.shape[1] // LANES):
            cs = slice(cb * LANES, (cb + 1) * LANES)
            acc = jnp.broadcast_to(dwb_ref[:, cs], (ls, LANES))
            for ph in range(SUBLANES):
                taps = [k for k in range(CONF_W) if (k + off) % SUBLANES == ph]
                if not taps:
                    continue
                span = max((k + off) // SUBLANES for k in taps) * SUBLANES + ls
                if ph:
                    view_scr[ph, 0:span, :] = win_ref[j, ph:ph + span, cs]
                for k in taps:
                    m = (k + off) // SUBLANES * SUBLANES
                    rows = view_scr[ph, m:m + ls, :] if ph else win_ref[j, m:m + ls, cs]
                    acc = acc + rows * dww_ref[k:k + 1, cs]
            c_scr[j * ls:(j + 1) * ls, cs] = acc
    c = c_scr[...]

    mu = jnp.mean(c, axis=-1, keepdims=True)
    xc = c - mu
    y = xc * lax.rsqrt(jnp.mean(xc * xc, axis=-1, keepdims=True) + EPS)
    y = y * lng_ref[...] + lnb_ref[...]
    y = y * _sigmoid(y)
    ya_ref[...] = _dot(y.astype(BF16), wout_ref[...]) + bout_ref[...]

    for j in range(nb):
        tail = win_ref[j, lv:lv + CONF_HIST, :]
        win_ref[j, 0:CONF_HIST, :] = tail
    hist_ref[...] = win_ref[:, 0:CONF_HIST, :]


def _conf_branch(proj, cache, p, *, row_off, nseq, nb, ls, lv, n_tiles, cache_bcast):
    c = p["dww"].shape[1]
    d = p["wout"].shape[1]
    r = nb * ls
    bsteps = nseq // nb
    blk0 = row_off // r
    row_map = lambda b, i: (blk0 + b * n_tiles + i, 0)
    cache_map = (lambda b, i: (0, 0, 0)) if cache_bcast else (lambda b, i: (b, 0, 0))
    const2 = lambda b, i: (0, 0)
    kern = functools.partial(_conf_kernel, nb=nb, ls=ls, lv=lv)
    ya, hist = pl.pallas_call(
        kern,
        grid=(bsteps, n_tiles),
        in_specs=[
            pl.BlockSpec((r, c), row_map),
            pl.BlockSpec((r, c), lambda b, i: (blk0 + b * n_tiles + i, 1)),
            pl.BlockSpec((nb, CONF_HIST, c), cache_map),
            pl.BlockSpec((CONF_HIST, c), const2),
            pl.BlockSpec((1, c), const2),
            pl.BlockSpec((1, c), const2),
            pl.BlockSpec((1, c), const2),
            pl.BlockSpec((c, d), const2),
            pl.BlockSpec((1, d), const2),
        ],
        out_specs=[
            pl.BlockSpec((r, d), lambda b, i: (b * n_tiles + i, 0)),
            pl.BlockSpec((nb, CONF_HIST, c), lambda b, i: (b, 0, 0)),
        ],
        out_shape=[
            jax.ShapeDtypeStruct((bsteps * n_tiles * r, d), F32),
            jax.ShapeDtypeStruct((nseq, CONF_HIST, c), F32),
        ],
        scratch_shapes=[pltpu.VMEM((nb, CONF_HIST + ls, c), F32), pltpu.VMEM((r, c), F32),
                        pltpu.VMEM((SUBLANES, CONF_HIST + ls, LANES), F32)],
        compiler_params=_cparams(("parallel", "arbitrary")),
        name="conformer_branch",
    )(proj, proj, cache, p["dww"], p["dwb"], p["lng"], p["lnb"], p["wout"], p["bout"])
    return ya, hist


def _ssd_kernel(xbc_ref, z_ref, dt_ref, mc_ref, h0_ref, cw_ref, cb_ref, dtb_ref, alog_ref, dsk_ref,
                nw_ref, wout_ref, yb_ref, hout_ref, mcout_ref, cwin_scr, y_scr,
                *, nb, ls, lv, nheads, hdim, ngroups, nstate):
    r = nb * ls
    di = nheads * hdim
    gn = ngroups * nstate
    hpg = nheads // ngroups
    gw = hpg * hdim
    ls_shift = int(math.log2(ls))
    i = pl.program_id(1)

    @pl.when(i == 0)
    def _():
        hout_ref[...] = h0_ref[...]
        cwin_scr[:, 0:MCONV_HIST, :] = mc_ref[...]

    xbc = xbc_ref[...]
    for j in range(nb):
        cwin_scr[j, MCONV_HIST:MCONV_HIST + ls, :] = xbc[j * ls:(j + 1) * ls, :]
    off = MCONV_HIST - (M_CONV_W - 1)
    parts = []
    for j in range(nb):
        acc = jnp.broadcast_to(cb_ref[...], (ls, cb_ref.shape[1]))
        for k in range(M_CONV_W):
            acc = acc + cwin_scr[j, off + k:off + k + ls, :] * cw_ref[k:k + 1, :]
        parts.append(acc)
    xc = parts[0] if nb == 1 else jnp.concatenate(parts, axis=0)
    xc = xc * _sigmoid(xc)
    for j in range(nb):
        tail = cwin_scr[j, lv:lv + MCONV_HIST, :]
        cwin_scr[j, 0:MCONV_HIST, :] = tail
    mcout_ref[...] = cwin_scr[:, 0:MCONV_HIST, :]

    xs = xc[:, :di]
    bm = xc[:, di:di + gn].astype(BF16)
    cm = xc[:, di + gn:].astype(BF16)

    row1 = lax.broadcasted_iota(jnp.int32, (r, 1), 0)
    valid = (row1 & (ls - 1)) < lv
    dt = jax.nn.softplus(dt_ref[...] + dtb_ref[...])
    dt = jnp.where(valid, dt, 0.0)
    a = dt * (-jnp.exp(alog_ref[...]))

    ri = lax.broadcasted_iota(jnp.int32, (r, r), 0)
    ci = lax.broadcasted_iota(jnp.int32, (r, r), 1)
    same = (ri >> ls_shift) == (ci >> ls_shift)
    causal = same & (ci <= ri)
    acs = _sel_right(causal.astype(BF16), a)
    tot = _sel_right(same.astype(BF16), a)
    acs_t = _transpose_id(acs, LANES)
    etot_t = jnp.exp(_transpose_id(tot, LANES))

    expand = (lax.broadcasted_iota(jnp.int32, (LANES, di), 0)
              == (lax.broadcasted_iota(jnp.int32, (LANES, di), 1) >> int(math.log2(hdim)))).astype(BF16)
    dt_rep = _sel_left(dt, expand)
    eacs_rep = _sel_left(jnp.exp(acs), expand)
    wend_rep = _sel_left(jnp.exp(tot - acs) * dt, expand)
    dx = (xs * dt_rep).astype(BF16)
    wdx = xs * wend_rep

    lane = lax.broadcasted_iota(jnp.int32, (r, LANES), 1)
    heads_per_blk = LANES // hdim
    for g in range(ngroups):
        bg = bm[:, g * nstate:(g + 1) * nstate]
        cg = cm[:, g * nstate:(g + 1) * nstate]
        cb = _dot_nt(cg, bg)
        for blk in range(hpg // heads_per_blk):
            col0 = g * gw + blk * LANES
            dxb = dx[:, col0:col0 + LANES]
            yacc = None
            for q in range(heads_per_blk):
                h = g * hpg + blk * heads_per_blk + q
                seg = acs[:, h:h + 1] - acs_t[h:h + 1, :]
                dec = jnp.exp(jnp.where(causal, seg, -jnp.inf))
                m = (cb * dec).astype(BF16)
                hm = (lane >= q * hdim) & (lane < (q + 1) * hdim)
                t = _dot(m, jnp.where(hm, dxb, jnp.zeros_like(dxb)))
                yacc = t if yacc is None else yacc + t
            y_scr[:, col0:col0 + LANES] = yacc

    col_l = lax.broadcasted_iota(jnp.int32, (r, LANES), 0)

    def seq_body(j, carry):
        if nb == 1:
            wdx_j = wdx
            mrow = None
        else:
            mrow = (row1 >> ls_shift) == j
            wdx_j = jnp.where(mrow, wdx, 0.0)
        wdx_j = wdx_j.astype(BF16)
        selc = (col_l == (j + 1) * ls - 1).astype(BF16)
        el = _sel_left(etot_t, selc)
        for g in range(ngroups):
            bg = bm[:, g * nstate:(g + 1) * nstate]
            cg = cm[:, g * nstate:(g + 1) * nstate]
            hg = jnp.concatenate([hout_ref[j, g * hpg + q] for q in range(hpg)], axis=0)
            yoff = _dot_nt(cg, hg.astype(BF16)) * eacs_rep[:, g * gw:(g + 1) * gw]
            if mrow is not None:
                yoff = jnp.where(mrow, yoff, 0.0)
            y_scr[:, g * gw:(g + 1) * gw] += yoff
            st = _dot_tn(wdx_j[:, g * gw:(g + 1) * gw], bg)
            for q in range(hpg):
                h = g * hpg + q
                scale = jnp.broadcast_to(el[h:h + 1, :], (hdim, nstate))
                hout_ref[j, h] = scale * hout_ref[j, h] + st[q * hdim:(q + 1) * hdim, :]
        return carry

    if nb == 1:
        seq_body(0, 0)
    else:
        lax.fori_loop(0, nb, seq_body, 0)

    y = y_scr[...] + dsk_ref[...] * xs
    z = z_ref[...]
    y = y * (z * _sigmoid(z))
    outs = []
    gsz = di // M_GROUPS
    for g in range(M_GROUPS):
        blk = y[:, g * gsz:(g + 1) * gsz]
        outs.append(blk * lax.rsqrt(jnp.mean(blk * blk, axis=-1, keepdims=True) + EPS))
    yn = jnp.concatenate(outs, axis=1) * nw_ref[...]
    yb_ref[...] = _dot(yn.astype(BF16), wout_ref[...])


def _ssd_branch(proj, mcache, h0, p, cols, *, row_off, nseq, nb, ls, lv, n_chunks, state_bcast):
    nheads, hdim, nstate = h0.shape[1], h0.shape[2], h0.shape[3]
    di = nheads * hdim
    cd = p["mcw"].shape[1]
    d = p["wmout"].shape[1]
    r = nb * ls
    bsteps = nseq // nb
    blk0 = row_off // r
    const2 = lambda b, i: (0, 0)
    st_map = (lambda b, i: (0, 0, 0, 0)) if state_bcast else (lambda b, i: (b, 0, 0, 0))
    mc_map = (lambda b, i: (0, 0, 0)) if state_bcast else (lambda b, i: (b, 0, 0))
    kern = functools.partial(_ssd_kernel, nb=nb, ls=ls, lv=lv, nheads=nheads, hdim=hdim,
                             ngroups=M_GROUPS, nstate=nstate)
    yb, hout, mcout = pl.pallas_call(
        kern,
        grid=(bsteps, n_chunks),
        in_specs=[
            pl.BlockSpec((r, cd), lambda b, i: (blk0 + b * n_chunks + i, cols["xbc"] // cd)),
            pl.BlockSpec((r, di), lambda b, i: (blk0 + b * n_chunks + i, cols["z"] // di)),
            pl.BlockSpec((r, LANES), lambda b, i: (blk0 + b * n_chunks + i, cols["dt"] // LANES)),
            pl.BlockSpec((nb, MCONV_HIST, cd), mc_map),
            pl.BlockSpec((nb, nheads, hdim, nstate), st_map),
            pl.BlockSpec((MCONV_HIST, cd), const2),
            pl.BlockSpec((1, cd), const2),
            pl.BlockSpec((1, LANES), const2),
            pl.BlockSpec((1, LANES), const2),
            pl.BlockSpec((1, di), const2),
            pl.BlockSpec((1, di), const2),
            pl.BlockSpec((di, d), const2),
        ],
        out_specs=[
            pl.BlockSpec((r, d), lambda b, i: (b * n_chunks + i, 0)),
            pl.BlockSpec((nb, nheads, hdim, nstate), lambda b, i: (b, 0, 0, 0)),
            pl.BlockSpec((nb, MCONV_HIST, cd), lambda b, i: (b, 0, 0)),
        ],
        out_shape=[
            jax.ShapeDtypeStruct((bsteps * n_chunks * r, d), F32),
            jax.ShapeDtypeStruct((nseq, nheads, hdim, nstate), F32),
            jax.ShapeDtypeStruct((nseq, MCONV_HIST, cd), F32),
        ],
        scratch_shapes=[
            pltpu.VMEM((nb, MCONV_HIST + ls, cd), F32),
            pltpu.VMEM((r, di), F32),
        ],
        compiler_params=_cparams(("parallel", "arbitrary")),
        name="ssd_branch",
    )(proj, proj, proj, mcache, h0, p["mcw"], p["mcb"], p["dtb"], p["alog"], p["dskip"],
      p["mnw"], p["wmout"])
    return yb, hout, mcout


def _topk_rows(s, k):
    n = s.shape[0]
    rows = lax.broadcasted_iota(jnp.int32, s.shape, 0).astype(F32)
    rank = jnp.full(s.shape, float(k), F32)
    vals = []
    for j in range(k):
        m = jnp.max(s, axis=0, keepdims=True)
        idx = jnp.min(jnp.where(s == m, rows, float(n)), axis=0, keepdims=True)
        hit = rows == idx
        rank = jnp.where(hit, float(j), rank)
        s = jnp.where(hit, -jnp.inf, s)
        vals.append(m)
    return vals, rank


def _cand_pairs(k):
    return [(a, j) for a in range(k) for j in range(k // (a + 1))]


def _gather_rows(src, idxs):
    sub = lax.broadcasted_iota(jnp.int32, (SUBLANES, src.shape[1]), 0)
    groups = {}
    for r, ix in enumerate(idxs):
        if ix is not None:
            groups.setdefault((ix // SUBLANES, (r - ix) % SUBLANES), []).append(r)
    out = None
    for (slab, shift), rs in groups.items():
        piece = src[slab * SUBLANES:(slab + 1) * SUBLANES]
        if shift:
            piece = pltpu.roll(piece, shift, 0)
        if out is None:
            out = piece
        else:
            m = functools.reduce(lambda p, q: p | q, [sub == r for r in rs])
            out = jnp.where(m, piece, out)
    return out


def _row_range_sum(x, lo, hi):
    if hi - lo == 1:
        return x[lo:lo + 1]
    sub = lax.broadcasted_iota(jnp.int32, (SUBLANES, x.shape[1]), 0)
    tot = None
    for sl in range(lo // SUBLANES, (hi - 1) // SUBLANES + 1):
        piece = x[sl * SUBLANES:(sl + 1) * SUBLANES]
        r0 = max(lo, sl * SUBLANES) - sl * SUBLANES
        r1 = min(hi, (sl + 1) * SUBLANES) - sl * SUBLANES
        if (r0, r1) != (0, SUBLANES):
            piece = jnp.where((sub >= r0) & (sub < r1), piece, 0.0)
        tot = piece if tot is None else tot + piece
    return jnp.sum(tot, axis=0, keepdims=True)


def _merge_route_kernel(x_ref, ya_ref, yb_ref, g_ref, wo_ref, n2_ref, wq_ref, keys_ref,
                        x1_ref, xn_ref, rta_ref, rtb_ref, *, nheads, nkeys, dhalf):
    d = x_ref.shape[1]
    g = _sigmoid(g_ref[...])
    mix = g[:, :d] * ya_ref[...] + g[:, d:] * yb_ref[...]
    x1 = x_ref[...] + _dot(mix.astype(BF16), wo_ref[...])
    x1_ref[...] = x1
    ms = jnp.mean(x1 * x1, axis=-1, keepdims=True)
    xn = (x1 * lax.rsqrt(ms + EPS) * n2_ref[...]).astype(BF16)
    xn_ref[...] = xn
    q = _dot(xn, wq_ref[...]).astype(BF16)

    k = PEER_TOPK
    for h in range(nheads):
        s1 = _dot_nt(keys_ref[h, 0], q[:, (2 * h) * dhalf:(2 * h + 1) * dhalf])
        s2 = _dot_nt(keys_ref[h, 1], q[:, (2 * h + 1) * dhalf:(2 * h + 2) * dhalf])
        v1, rank1 = _topk_rows(s1, k)
        v2, rank2 = _topk_rows(s2, k)
        v1m = jnp.concatenate(v1, axis=0)
        v2m = jnp.concatenate(v2, axis=0)
        pairs = _cand_pairs(k)
        nslab = -(-len(pairs) // SUBLANES)
        padded = pairs + [None] * (nslab * SUBLANES - len(pairs))
        slabs = []
        for sl in range(nslab):
            chunk = padded[sl * SUBLANES:(sl + 1) * SUBLANES]
            cs = (_gather_rows(v1m, [None if pr is None else pr[0] for pr in chunk])
                  + _gather_rows(v2m, [None if pr is None else pr[1] for pr in chunk]))
            slabs.append(cs)
        cand = jnp.concatenate(slabs, axis=0)
        rows = lax.broadcasted_iota(jnp.int32, cand.shape, 0).astype(F32)
        cand = jnp.where(rows < float(len(pairs)), cand, -jnp.inf)
        c = cand
        for _ in range(k):
            m = jnp.max(c, axis=0, keepdims=True)
            idx = jnp.min(jnp.where(c == m, rows, float(cand.shape[0])), axis=0, keepdims=True)
            c = jnp.where(rows == idx, -jnp.inf, c)
        sel = c != cand
        cmax = v1[0] + v2[0]
        zsum = jnp.sum(jnp.where(sel, jnp.exp(cand - cmax), 0.0), axis=0, keepdims=True)
        self32 = sel.astype(F32)
        cn = jnp.zeros(s1.shape, F32)
        for a in range(k):
            lo = pairs.index((a, 0))
            hi = lo + k // (a + 1)
            cn = cn + jnp.where(rank1 == float(a), _row_range_sum(self32, lo, hi), 0.0)
        rta_ref[h, 0] = cn
        rta_ref[h, 1] = jnp.exp(s1 - v1[0]) / zsum
        rtb_ref[h, 0] = rank2.astype(BF16)
        rtb_ref[h, 1] = jnp.exp(s2 - v2[0]).astype(BF16)


def _merge_route(x, ya, yb, proj, p, cols, tm):
    n, d = x.shape
    nheads, _, nkeys, dhalf = p["keys"].shape
    dq = p["wq"].shape[1]
    const2 = lambda i: (0, 0)
    kern = functools.partial(_merge_route_kernel, nheads=nheads, nkeys=nkeys, dhalf=dhalf)
    return pl.pallas_call(
        kern,
        grid=(n // tm,),
        in_specs=[
            pl.BlockSpec((tm, d), lambda i: (i, 0)),
            pl.BlockSpec((tm, d), lambda i: (i, 0)),
            pl.BlockSpec((tm, d), lambda i: (i, 0)),
            pl.BlockSpec((tm, 2 * d), lambda i: (i, cols["gate"] // (2 * d))),
            pl.BlockSpec((d, d), const2),
            pl.BlockSpec((1, d), const2),
            pl.BlockSpec((d, dq), const2),
            pl.BlockSpec((nheads, 2, nkeys, dhalf), lambda i: (0, 0, 0, 0)),
        ],
        out_specs=[
            pl.BlockSpec((tm, d), lambda i: (i, 0)),
            pl.BlockSpec((tm, d), lambda i: (i, 0)),
            pl.BlockSpec((nheads, 2, nkeys, tm), lambda i: (0, 0, 0, i)),
            pl.BlockSpec((nheads, 2, nkeys, tm), lambda i: (0, 0, 0, i)),
        ],
        out_shape=[
            jax.ShapeDtypeStruct((n, d), F32),
            jax.ShapeDtypeStruct((n, d), BF16),
            jax.ShapeDtypeStruct((nheads, 2, nkeys, n), F32),
            jax.ShapeDtypeStruct((nheads, 2, nkeys, n), BF16),
        ],
        compiler_params=_cparams(("parallel",)),
        name="merge_route",
    )(x, ya, yb, proj, p["wo"], p["n2"], p["wq"], p["keys"])


def _peer_kernel(xn_ref, rta_ref, rtb_ref, u_ref, vt_ref, x1_ref, fw_ref, o_ref,
                 acc_ref, ht0, ht1, wt0, wt1, *, nheads, nkeys, nblk, final_norm):
    f = pl.program_id(0)
    eb = u_ref.shape[0]
    na = eb // nkeys
    e_b = lax.rem(jnp.maximum(f - 1, 0), nblk)
    e_c = lax.rem(jnp.maximum(f - 2, 0), nblk)

    @pl.when(f == 0)
    def _():
        for r in (ht0, ht1, wt0, wt1):
            r[...] = jnp.zeros_like(r)

    @pl.when(e_c == 0)
    def _():
        acc_ref[...] = jnp.zeros_like(acc_ref)

    def stages(ht_w, ht_r, wt_w, wt_r):
        ht_w[...] = _dot_nt(u_ref[...], xn_ref[...])
        a0 = pl.multiple_of(e_b * na, na)
        slabs = [(rta_ref[h, 0, pl.ds(a0, na), :], rta_ref[h, 1, pl.ds(a0, na), :])
                 for h in range(nheads)]
        tt = xn_ref.shape[0]
        nv = nkeys // BF16_ROWS
        for al in range(na):
            gates = [None] * nv
            for h in range(nheads):
                cn = jnp.broadcast_to(slabs[h][0][al:al + 1, :], (BF16_ROWS, tt)).astype(BF16)
                e1 = jnp.broadcast_to(slabs[h][1][al:al + 1, :], (BF16_ROWS, tt)).astype(BF16)
                for v in range(nv):
                    rk = rtb_ref[h, 0, v * BF16_ROWS:(v + 1) * BF16_ROWS, :]
                    p2 = rtb_ref[h, 1, v * BF16_ROWS:(v + 1) * BF16_ROWS, :]
                    t = jnp.where(rk < cn, p2, jnp.zeros_like(p2)) * e1
                    gates[v] = t if gates[v] is None else gates[v] + t
            for v in range(nv):
                r0 = al * nkeys + v * BF16_ROWS
                hv = ht_r[r0:r0 + BF16_ROWS, :]
                act = 0.5 * hv * (1.0 + lax.erf(hv * (1.0 / math.sqrt(2.0))))
                wt_w[r0:r0 + BF16_ROWS, :] = (gates[v].astype(F32) * act).astype(BF16)
        acc_ref[...] += _dot(vt_ref[...], wt_r[...])

    even = lax.rem(f, 2) == 0

    @pl.when(even)
    def _():
        stages(ht0, ht1, wt1, wt0)

    @pl.when(jnp.logical_not(even))
    def _():
        stages(ht1, ht0, wt0, wt1)

    @pl.when((f >= 2) & (e_c == nblk - 1))
    def _():
        x2 = x1_ref[...] + acc_ref[...].T
        if final_norm:
            ms = jnp.mean(x2 * x2, axis=-1, keepdims=True)
            x2 = x2 * lax.rsqrt(ms + EPS) * fw_ref[...]
        o_ref[...] = x2


def _peer(xn, rta, rtb, u_bf, vt_bf, x1, fw, tt, eb, final_norm):
    n, d = x1.shape
    ne = u_bf.shape[0]
    nheads, _, nkeys, _ = rta.shape
    nblk = ne // eb
    nflat = (n // tt) * nblk
    fa = lambda f: jnp.minimum(f, nflat - 1)
    fb = lambda f: jnp.clip(f - 1, 0, nflat - 1)
    fc = lambda f: jnp.clip(f - 2, 0, nflat - 1)
    kern = functools.partial(_peer_kernel, nheads=nheads, nkeys=nkeys, nblk=nblk,
                             final_norm=final_norm)
    return pl.pallas_call(
        kern,
        grid=(nflat + 2,),
        in_specs=[
            pl.BlockSpec((tt, d), lambda f: (fa(f) // nblk, 0)),
            pl.BlockSpec((nheads, 2, nkeys, tt), lambda f: (0, 0, 0, fb(f) // nblk)),
            pl.BlockSpec((nheads, 2, nkeys, tt), lambda f: (0, 0, 0, fb(f) // nblk)),
            pl.BlockSpec((eb, d), lambda f: (fa(f) % nblk, 0)),
            pl.BlockSpec((d, eb), lambda f: (0, fc(f) % nblk)),
            pl.BlockSpec((tt, d), lambda f: (fc(f) // nblk, 0)),
            pl.BlockSpec((1, d), lambda f: (0, 0)),
        ],
        out_specs=pl.BlockSpec((tt, d), lambda f: (fc(f) // nblk, 0)),
        out_shape=jax.ShapeDtypeStruct((n, d), F32),
        scratch_shapes=[
            pltpu.VMEM((d, tt), F32),
            pltpu.VMEM((eb, tt), F32), pltpu.VMEM((eb, tt), F32),
            pltpu.VMEM((eb, tt), BF16), pltpu.VMEM((eb, tt), BF16),
        ],
        compiler_params=_cparams(("arbitrary",)),
        name="peer_experts",
    )(xn, rta, rtb, u_bf, vt_bf, x1, fw)


def _pad_rows(a, n):
    return jnp.pad(a, ((0, n - a.shape[0]),) + ((0, 0),) * (a.ndim - 1))


def kernel(x_prompt, x_sample, cache_conf, cache_mconv, state_ssm, meta_tokens, norm1_w, w_in, b_in, conf_dw_w, conf_dw_b, conf_ln_g, conf_ln_b, w_conf_out, b_conf_out, m_conv_w, m_conv_b, dt_bias, A_log, D_skip, m_norm_w, w_m_out, w_o, norm2_w, peer_w_q, peer_keys, peer_u, peer_v, final_norm_w):
    bp, sp, d = x_prompt.shape
    db, ds, _ = x_sample.shape
    depth = norm1_w.shape[0]
    c_conv = conf_dw_w.shape[2]
    nheads = dt_bias.shape[1]
    di = nheads * M_HEADDIM
    gn = M_GROUPS * M_STATE
    cd = di + 2 * gn
    n_p, n_s = bp * sp, db * ds
    n_all = -(-(n_p + n_s + META_ROWS) // ROW_ALIGN) * ROW_ALIGN
    off_s, off_m = n_p, n_p + n_s

    cols = {"glu_a": 0, "glu_b": c_conv, "z": 2 * c_conv, "xbc": 2 * c_conv + di}
    assert cols["z"] % di == 0 and cols["xbc"] % cd == 0
    cols["gate"] = cols["xbc"] + cd
    assert cols["gate"] % (2 * d) == 0
    cols["dt"] = cols["gate"] + 2 * d
    ncol = cols["dt"] + LANES
    src_dt = 2 * c_conv + di + cd
    src_gate = src_dt + nheads
    tn = ncol // 9 if (ncol % 9 == 0 and (ncol // 9) % LANES == 0) else LANES
    lin_tm = LIN_ROWS if n_all % LIN_ROWS == 0 else ROW_ALIGN

    seq_nb = 8
    prompt_ls = 128
    assert sp % prompt_ls == 0 and db % seq_nb == 0 and n_p % LANES == 0 and n_s % LANES == 0

    x_all = jnp.concatenate([
        x_prompt.reshape(n_p, d), x_sample.reshape(n_s, d), meta_tokens.astype(F32),
        jnp.zeros((n_all - n_p - n_s - N_META, d), F32)], axis=0)

    outs = {k: [] for k in ("conf_p", "mconv_p", "ssm_p", "conf_s", "mconv_s", "ssm_s")}
    for l in range(depth):
        wl, bl = w_in[l], b_in[l]
        w_re = jnp.concatenate([wl[:, :src_dt], wl[:, src_gate:], wl[:, src_dt:src_gate],
                                jnp.zeros((d, LANES - nheads), F32)], axis=1).astype(BF16)
        b_re = jnp.concatenate([bl[:src_dt], bl[src_gate:], bl[src_dt:src_gate],
                                jnp.zeros((LANES - nheads,), F32)])[None, :]
        proj = _norm_linear(x_all, norm1_w[l][None, :], w_re, b_re, lin_tm, tn)

        pc = {
            "dww": _pad_rows(conf_dw_w[l], CONF_HIST), "dwb": conf_dw_b[l][None, :],
            "lng": conf_ln_g[l][None, :], "lnb": conf_ln_b[l][None, :],
            "wout": w_conf_out[l].astype(BF16), "bout": b_conf_out[l][None, :],
        }
        ps = {
            "mcw": _pad_rows(m_conv_w[l], MCONV_HIST), "mcb": m_conv_b[l][None, :],
            "dtb": jnp.pad(dt_bias[l], (0, LANES - nheads))[None, :],
            "alog": jnp.pad(A_log[l], (0, LANES - nheads))[None, :],
            "dskip": jnp.repeat(D_skip[l], M_HEADDIM)[None, :],
            "mnw": m_norm_w[l][None, :], "wmout": w_m_out[l].astype(BF16),
        }

        ya_m, cf_m = _conf_branch(proj, jnp.zeros((1, CONF_HIST, c_conv), F32), pc, row_off=off_m,
                                  nseq=1, nb=1, ls=META_ROWS, lv=N_META, n_tiles=1, cache_bcast=False)
        yb_m, h_m, mc_m = _ssd_branch(proj, jnp.zeros((1, MCONV_HIST, cd), F32),
                                      jnp.zeros((1, nheads, M_HEADDIM, M_STATE), F32), ps, cols,
                                      row_off=off_m, nseq=1, nb=1, ls=META_ROWS, lv=N_META,
                                      n_chunks=1, state_bcast=False)
        ya_p, cf_p = _conf_branch(proj, cf_m, pc, row_off=0, nseq=bp, nb=1, ls=prompt_ls, lv=prompt_ls,
                                  n_tiles=sp // prompt_ls, cache_bcast=True)
        yb_p, h_p, mc_p = _ssd_branch(proj, mc_m, h_m, ps, cols, row_off=0, nseq=bp, nb=1,
                                      ls=prompt_ls, lv=prompt_ls, n_chunks=sp // prompt_ls,
                                      state_bcast=True)
        cf_in = jnp.pad(cache_conf[l], ((0, 0), (CONF_HIST - (CONF_W - 1), 0), (0, 0)))
        mc_in = jnp.pad(cache_mconv[l], ((0, 0), (MCONV_HIST - (M_CONV_W - 1), 0), (0, 0)))
        ya_s, cf_s = _conf_branch(proj, cf_in, pc, row_off=off_s, nseq=db, nb=seq_nb, ls=ds, lv=ds,
                                  n_tiles=1, cache_bcast=False)
        yb_s, h_s, mc_s = _ssd_branch(proj, mc_in, state_ssm[l], ps, cols, row_off=off_s, nseq=db,
                                      nb=seq_nb, ls=ds, lv=ds, n_chunks=1, state_bcast=False)

        tail = jnp.zeros((n_all - off_m - META_ROWS, d), F32)
        ya = jnp.concatenate([ya_p, ya_s, ya_m, tail], axis=0)
        yb = jnp.concatenate([yb_p, yb_s, yb_m, tail], axis=0)

        pm = {"wo": w_o[l].astype(BF16), "n2": norm2_w[l][None, :], "wq": peer_w_q[l].astype(BF16),
              "keys": peer_keys[l].astype(BF16)}
        x1, xn2, rta, rtb = _merge_route(x_all, ya, yb, proj, pm, cols, 256)
        x_all = _peer(xn2, rta, rtb, peer_u[l].astype(BF16), peer_v[l].T.astype(BF16), x1,
                      final_norm_w[None, :], 256, 8 * peer_keys.shape[3], l == depth - 1)

        outs["conf_p"].append(cf_p[:, CONF_HIST - (CONF_W - 1):])
        outs["mconv_p"].append(mc_p[:, MCONV_HIST - (M_CONV_W - 1):])
        outs["ssm_p"].append(h_p)
        outs["conf_s"].append(cf_s[:, CONF_HIST - (CONF_W - 1):])
        outs["mconv_s"].append(mc_s[:, MCONV_HIST - (M_CONV_W - 1):])
        outs["ssm_s"].append(h_s)

    y_prompt = x_all[:n_p].reshape(bp, sp, d)
    y_sample = x_all[off_s:off_s + n_s].reshape(db, ds, d)
    return (y_prompt, y_sample,
            jnp.stack(outs["conf_p"]), jnp.stack(outs["mconv_p"]), jnp.stack(outs["ssm_p"]),
            jnp.stack(outs["conf_s"]), jnp.stack(outs["mconv_s"]), jnp.stack(outs["ssm_s"]))
```

```python
import functools
import math

import jax
import jax.numpy as jnp
from jax import lax
from jax.experimental import pallas as pl
from jax.experimental.pallas import tpu as pltpu

F32 = jnp.float32
BF16 = jnp.bfloat16

EPS = 1e-6
N_META = 16
CONF_W = 31
M_CONV_W = 4
M_HEADDIM = 64
M_GROUPS = 8
M_STATE = 128
PEER_TOPK = 16
LANES = 128
SUBLANES = 8
BF16_ROWS = 16
CONF_HIST = 32
MCONV_HIST = 8
META_ROWS = 128
ROW_ALIGN = 512
LIN_ROWS = 1280
ROUTE_ROWS = 256
PEER_ROWS = 512
PEER_KEY_ROWS = 8
VMEM_LIMIT = 56 * 1024 * 1024


def _cparams(sem):
    return pltpu.CompilerParams(dimension_semantics=sem, vmem_limit_bytes=VMEM_LIMIT)


def _split3(a):
    p1 = a.astype(BF16)
    r1 = a - p1.astype(F32)
    p2 = r1.astype(BF16)
    r2 = r1 - p2.astype(F32)
    p3 = r2.astype(BF16)
    return p1, p2, p3


def _dot(a, b):
    return jnp.dot(a, b, preferred_element_type=F32)


def _dot_nt(a, b):
    return lax.dot_general(a, b, (((1,), (1,)), ((), ())), preferred_element_type=F32)


def _dot_tn(a, b):
    return lax.dot_general(a, b, (((0,), (0,)), ((), ())), preferred_element_type=F32)


def _sel_right(m01, a):
    p1, p2, p3 = _split3(a)
    return _dot(m01, p1) + _dot(m01, p2) + _dot(m01, p3)


def _sel_left(a, m01):
    p1, p2, p3 = _split3(a)
    return _dot(p1, m01) + _dot(p2, m01) + _dot(p3, m01)


def _transpose_id(a, n):
    eye = (lax.broadcasted_iota(jnp.int32, (n, n), 0)
           == lax.broadcasted_iota(jnp.int32, (n, n), 1)).astype(BF16)
    p1, p2, p3 = _split3(a)
    return _dot_nt(eye, p1) + _dot_nt(eye, p2) + _dot_nt(eye, p3)


def _sigmoid(x):
    return 1.0 / (1.0 + jnp.exp(-x))


def _lin_kernel(x_ref, nw_ref, w_ref, b_ref, o_ref, xn_ref):
    @pl.when(pl.program_id(1) == 0)
    def _():
        x = x_ref[...]
        ms = jnp.mean(x * x, axis=-1, keepdims=True)
        xn_ref[...] = (x * lax.rsqrt(ms + EPS) * nw_ref[...]).astype(BF16)

    o_ref[...] = _dot(xn_ref[...], w_ref[...]) + b_ref[...]


def _norm_linear(x, nw, w, b, tm, tn):
    n, d = x.shape
    nc = w.shape[1]
    return pl.pallas_call(
        _lin_kernel,
        grid=(n // tm, nc // tn),
        in_specs=[
            pl.BlockSpec((tm, d), lambda i, j: (i, 0)),
            pl.BlockSpec((1, d), lambda i, j: (0, 0)),
            pl.BlockSpec((d, tn), lambda i, j: (0, j)),
            pl.BlockSpec((1, tn), lambda i, j: (0, j)),
        ],
        out_specs=pl.BlockSpec((tm, tn), lambda i, j: (i, j)),
        out_shape=jax.ShapeDtypeStruct((n, nc), F32),
        scratch_shapes=[pltpu.VMEM((tm, d), BF16)],
        compiler_params=_cparams(("parallel", "arbitrary")),
        name="norm_linear",
    )(x, nw, w, b)


def _conf_kernel(ga_ref, gb_ref, cache_ref, dww_ref, dwb_ref, lng_ref, lnb_ref, wout_ref, bout_ref,
                 ya_ref, hist_ref, win_ref, c_scr, view_scr, *, nb, ls, lv):
    @pl.when(pl.program_id(1) == 0)
    def _():
        win_ref[:, 0:CONF_HIST, :] = cache_ref[...]

    u = ga_ref[...] * _sigmoid(gb_ref[...])
    for j in range(nb):
        win_ref[j, CONF_HIST:CONF_HIST + ls, :] = u[j * ls:(j + 1) * ls, :]

    off = CONF_HIST - (CONF_W - 1)
    for j in range(nb):
        for cb in range(dwb_ref.shape[1] // LANES):
            cs = slice(cb * LANES, (cb + 1) * LANES)
            acc = jnp.broadcast_to(dwb_ref[:, cs], (ls, LANES))
            for ph in range(SUBLANES):
                taps = [k for k in range(CONF_W) if (k + off) % SUBLANES == ph]
                if not taps:
                    continue
                span = max((k + off) // SUBLANES for k in taps) * SUBLANES + ls
                if ph:
                    view_scr[ph, 0:span, :] = win_ref[j, ph:ph + span, cs]
                for k in taps:
                    m = (k + off) // SUBLANES * SUBLANES
                    rows = view_scr[ph, m:m + ls, :] if ph else win_ref[j, m:m + ls, cs]
                    acc = acc + rows * dww_ref[k:k + 1, cs]
            c_scr[j * ls:(j + 1) * ls, cs] = acc
    c = c_scr[...]

    mu = jnp.mean(c, axis=-1, keepdims=True)
    xc = c - mu
    y = xc * lax.rsqrt(jnp.mean(xc * xc, axis=-1, keepdims=True) + EPS)
    y = y * lng_ref[...] + lnb_ref[...]
    y = y * _sigmoid(y)
    ya_ref[...] = _dot(y.astype(BF16), wout_ref[...]) + bout_ref[...]

    for j in range(nb):
        tail = win_ref[j, lv:lv + CONF_HIST, :]
        win_ref[j, 0:CONF_HIST, :] = tail
    hist_ref[...] = win_ref[:, 0:CONF_HIST, :]


def _skip_ref(kern, pos):
    def wrapped(*refs):
        return kern(*refs[:pos], *refs[pos + 1:])
    return wrapped


def _conf_branch(proj, cache, p, y_prev, *, row_off, nseq, nb, ls, lv, n_tiles, cache_bcast):
    c = p["dww"].shape[1]
    d = p["wout"].shape[1]
    r = nb * ls
    bsteps = nseq // nb
    blk0 = row_off // r
    row_map = lambda b, i: (blk0 + b * n_tiles + i, 0)
    cache_map = (lambda b, i: (0, 0, 0)) if cache_bcast else (lambda b, i: (b, 0, 0))
    const2 = lambda b, i: (0, 0)
    kern = functools.partial(_conf_kernel, nb=nb, ls=ls, lv=lv)
    in_specs = [
        pl.BlockSpec((r, c), row_map),
        pl.BlockSpec((r, c), lambda b, i: (blk0 + b * n_tiles + i, 1)),
        pl.BlockSpec((nb, CONF_HIST, c), cache_map),
        pl.BlockSpec((CONF_HIST, c), const2),
        pl.BlockSpec((1, c), const2),
        pl.BlockSpec((1, c), const2),
        pl.BlockSpec((1, c), const2),
        pl.BlockSpec((c, d), const2),
        pl.BlockSpec((1, d), const2),
    ]
    args = [proj, proj, cache, p["dww"], p["dwb"], p["lng"], p["lnb"], p["wout"], p["bout"]]
    aliases = {}
    if y_prev is not None:
        aliases = {len(args): 0}
        kern = _skip_ref(kern, len(args))
        in_specs.append(pl.BlockSpec(memory_space=pl.ANY))
        args.append(y_prev)
    ya, hist = pl.pallas_call(
        kern,
        grid=(bsteps, n_tiles),
        in_specs=in_specs,
        out_specs=[
            pl.BlockSpec((r, d), row_map),
            pl.BlockSpec((nb, CONF_HIST, c), lambda b, i: (b, 0, 0)),
        ],
        out_shape=[
            jax.ShapeDtypeStruct((proj.shape[0], d), F32),
            jax.ShapeDtypeStruct((nseq, CONF_HIST, c), F32),
        ],
        scratch_shapes=[pltpu.VMEM((nb, CONF_HIST + ls, c), F32), pltpu.VMEM((r, c), F32),
                        pltpu.VMEM((SUBLANES, CONF_HIST + ls, LANES), F32)],
        input_output_aliases=aliases,
        compiler_params=_cparams(("parallel", "arbitrary")),
        name="conformer_branch",
    )(*args)
    return ya, hist


def _ssd_kernel(xbc_ref, z_ref, dt_ref, mc_ref, h0_ref, cw_ref, cb_ref, dtb_ref, alog_ref, dsk_ref,
                nw_ref, wout_ref, yb_ref, hout_ref, mcout_ref, cwin_scr, y_scr,
                *, nb, ls, lv, nheads, hdim, ngroups, nstate):
    r = nb * ls
    di = nheads * hdim
    gn = ngroups * nstate
    hpg = nheads // ngroups
    gw = hpg * hdim
    ls_shift = int(math.log2(ls))
    i = pl.program_id(1)

    @pl.when(i == 0)
    def _():
        hout_ref[...] = h0_ref[...]
        cwin_scr[:, 0:MCONV_HIST, :] = mc_ref[...]

    xbc = xbc_ref[...]
    for j in range(nb):
        cwin_scr[j, MCONV_HIST:MCONV_HIST + ls, :] = xbc[j * ls:(j + 1) * ls, :]
    off = MCONV_HIST - (M_CONV_W - 1)
    parts = []
    for j in range(nb):
        acc = jnp.broadcast_to(cb_ref[...], (ls, cb_ref.shape[1]))
        for k in range(M_CONV_W):
            acc = acc + cwin_scr[j, off + k:off + k + ls, :] * cw_ref[k:k + 1, :]
        parts.append(acc)
    xc = parts[0] if nb == 1 else jnp.concatenate(parts, axis=0)
    xc = xc * _sigmoid(xc)
    for j in range(nb):
        tail = cwin_scr[j, lv:lv + MCONV_HIST, :]
        cwin_scr[j, 0:MCONV_HIST, :] = tail
    mcout_ref[...] = cwin_scr[:, 0:MCONV_HIST, :]

    xs = xc[:, :di]
    bm = xc[:, di:di + gn].astype(BF16)
    cm = xc[:, di + gn:].astype(BF16)

    row1 = lax.broadcasted_iota(jnp.int32, (r, 1), 0)
    valid = (row1 & (ls - 1)) < lv
    dt = jax.nn.softplus(dt_ref[...] + dtb_ref[...])
    dt = jnp.where(valid, dt, 0.0)
    a = dt * (-jnp.exp(alog_ref[...]))

    ri = lax.broadcasted_iota(jnp.int32, (r, r), 0)
    ci = lax.broadcasted_iota(jnp.int32, (r, r), 1)
    same = (ri >> ls_shift) == (ci >> ls_shift)
    causal = same & (ci <= ri)
    acs = _sel_right(causal.astype(BF16), a)
    tot = _sel_right(same.astype(BF16), a)
    acs_t = _transpose_id(acs, LANES)
    etot_t = jnp.exp(_transpose_id(tot, LANES))

    expand = (lax.broadcasted_iota(jnp.int32, (LANES, di), 0)
              == (lax.broadcasted_iota(jnp.int32, (LANES, di), 1) >> int(math.log2(hdim)))).astype(BF16)
    dt_rep = _sel_left(dt, expand)
    eacs_rep = _sel_left(jnp.exp(acs), expand)
    wend_rep = _sel_left(jnp.exp(tot - acs) * dt, expand)
    dx = (xs * dt_rep).astype(BF16)
    wdx = xs * wend_rep

    lane = lax.broadcasted_iota(jnp.int32, (r, LANES), 1)
    heads_per_blk = LANES // hdim
    for g in range(ngroups):
        bg = bm[:, g * nstate:(g + 1) * nstate]
        cg = cm[:, g * nstate:(g + 1) * nstate]
        cb = _dot_nt(cg, bg)
        for blk in range(hpg // heads_per_blk):
            col0 = g * gw + blk * LANES
            dxb = dx[:, col0:col0 + LANES]
            yacc = None
            for q in range(heads_per_blk):
                h = g * hpg + blk * heads_per_blk + q
                seg = acs[:, h:h + 1] - acs_t[h:h + 1, :]
                dec = jnp.exp(jnp.where(causal, seg, -jnp.inf))
                m = (cb * dec).astype(BF16)
                hm = (lane >= q * hdim) & (lane < (q + 1) * hdim)
                t = _dot(m, jnp.where(hm, dxb, jnp.zeros_like(dxb)))
                yacc = t if yacc is None else yacc + t
            y_scr[:, col0:col0 + LANES] = yacc

    col_l = lax.broadcasted_iota(jnp.int32, (r, LANES), 0)

    def seq_body(j, carry):
        if nb == 1:
            wdx_j = wdx
            mrow = None
        else:
            mrow = (row1 >> ls_shift) == j
            wdx_j = jnp.where(mrow, wdx, 0.0)
        wdx_j = wdx_j.astype(BF16)
        selc = (col_l == (j + 1) * ls - 1).astype(BF16)
        el = _sel_left(etot_t, selc)
        for g in range(ngroups):
            bg = bm[:, g * nstate:(g + 1) * nstate]
            cg = cm[:, g * nstate:(g + 1) * nstate]
            hg = jnp.concatenate([hout_ref[j, g * hpg + q] for q in range(hpg)], axis=0)
            yoff = _dot_nt(cg, hg.astype(BF16)) * eacs_rep[:, g * gw:(g + 1) * gw]
            if mrow is not None:
                yoff = jnp.where(mrow, yoff, 0.0)
            y_scr[:, g * gw:(g + 1) * gw] += yoff
            st = _dot_tn(wdx_j[:, g * gw:(g + 1) * gw], bg)
            for q in range(hpg):
                h = g * hpg + q
                scale = jnp.broadcast_to(el[h:h + 1, :], (hdim, nstate))
                hout_ref[j, h] = scale * hout_ref[j, h] + st[q * hdim:(q + 1) * hdim, :]
        return carry

    if nb == 1:
        seq_body(0, 0)
    else:
        lax.fori_loop(0, nb, seq_body, 0)

    y = y_scr[...] + dsk_ref[...] * xs
    z = z_ref[...]
    y = y * (z * _sigmoid(z))
    outs = []
    gsz = di // M_GROUPS
    for g in range(M_GROUPS):
        blk = y[:, g * gsz:(g + 1) * gsz]
        outs.append(blk * lax.rsqrt(jnp.mean(blk * blk, axis=-1, keepdims=True) + EPS))
    yn = jnp.concatenate(outs, axis=1) * nw_ref[...]
    yb_ref[...] = _dot(yn.astype(BF16), wout_ref[...])


def _ssd_branch(proj, mcache, h0, p, cols, y_prev, hs_prev, *, h0_layer, out_layer, out_depth,
                row_off, nseq, nb, ls, lv, n_chunks, state_bcast):
    nheads, hdim, nstate = h0.shape[2], h0.shape[3], h0.shape[4]
    di = nheads * hdim
    cd = p["mcw"].shape[1]
    d = p["wmout"].shape[1]
    r = nb * ls
    bsteps = nseq // nb
    blk0 = row_off // r
    const2 = lambda b, i: (0, 0)
    st_map = ((lambda b, i: (h0_layer, 0, 0, 0, 0)) if state_bcast
              else (lambda b, i: (h0_layer, b, 0, 0, 0)))
    mc_map = (lambda b, i: (0, 0, 0)) if state_bcast else (lambda b, i: (b, 0, 0))
    kern = functools.partial(_ssd_kernel, nb=nb, ls=ls, lv=lv, nheads=nheads, hdim=hdim,
                             ngroups=M_GROUPS, nstate=nstate)
    in_specs = [
        pl.BlockSpec((r, cd), lambda b, i: (blk0 + b * n_chunks + i, cols["xbc"] // cd)),
        pl.BlockSpec((r, di), lambda b, i: (blk0 + b * n_chunks + i, cols["z"] // di)),
        pl.BlockSpec((r, LANES), lambda b, i: (blk0 + b * n_chunks + i, cols["dt"] // LANES)),
        pl.BlockSpec((nb, MCONV_HIST, cd), mc_map),
        pl.BlockSpec((None, nb, nheads, hdim, nstate), st_map),
        pl.BlockSpec((MCONV_HIST, cd), const2),
        pl.BlockSpec((1, cd), const2),
        pl.BlockSpec((1, LANES), const2),
        pl.BlockSpec((1, LANES), const2),
        pl.BlockSpec((1, di), const2),
        pl.BlockSpec((1, di), const2),
        pl.BlockSpec((di, d), const2),
    ]
    args = [proj, proj, proj, mcache, h0, p["mcw"], p["mcb"], p["dtb"], p["alog"], p["dskip"],
            p["mnw"], p["wmout"]]
    aliases = {}
    n_in = len(args)
    for prev, out_idx in ((y_prev, 0), (hs_prev, 1)):
        if prev is not None:
            aliases[len(args)] = out_idx
            in_specs.append(pl.BlockSpec(memory_space=pl.ANY))
            args.append(prev)
    for pos in range(n_in, len(args)):
        kern = _skip_ref(kern, pos)
    yb, hout, mcout = pl.pallas_call(
        kern,
        grid=(bsteps, n_chunks),
        in_specs=in_specs,
        out_specs=[
            pl.BlockSpec((r, d), lambda b, i: (blk0 + b * n_chunks + i, 0)),
            pl.BlockSpec((None, nb, nheads, hdim, nstate), lambda b, i: (out_layer, b, 0, 0, 0)),
            pl.BlockSpec((nb, MCONV_HIST, cd), lambda b, i: (b, 0, 0)),
        ],
        out_shape=[
            jax.ShapeDtypeStruct((proj.shape[0], d), F32),
            jax.ShapeDtypeStruct((out_depth, nseq, nheads, hdim, nstate), F32),
            jax.ShapeDtypeStruct((nseq, MCONV_HIST, cd), F32),
        ],
        scratch_shapes=[
            pltpu.VMEM((nb, MCONV_HIST + ls, cd), F32),
            pltpu.VMEM((r, di), F32),
        ],
        input_output_aliases=aliases,
        compiler_params=_cparams(("parallel", "arbitrary")),
        name="ssd_branch",
    )(*args)
    return yb, hout, mcout


def _topk_rows(s, k):
    n = s.shape[0]
    rows = lax.broadcasted_iota(jnp.int32, s.shape, 0).astype(F32)
    rank = jnp.full(s.shape, float(k), F32)
    vals = []
    for j in range(k):
        m = jnp.max(s, axis=0, keepdims=True)
        idx = jnp.min(jnp.where(s == m, rows, float(n)), axis=0, keepdims=True)
        hit = rows == idx
        rank = jnp.where(hit, float(j), rank)
        s = jnp.where(hit, -jnp.inf, s)
        vals.append(m)
    return vals, rank


def _cand_pairs(k):
    return [(a, j) for a in range(k) for j in range(k // (a + 1))]


def _gather_rows(src, idxs):
    sub = lax.broadcasted_iota(jnp.int32, (SUBLANES, src.shape[1]), 0)
    groups = {}
    for r, ix in enumerate(idxs):
        if ix is not None:
            groups.setdefault((ix // SUBLANES, (r - ix) % SUBLANES), []).append(r)
    out = None
    for (slab, shift), rs in groups.items():
        piece = src[slab * SUBLANES:(slab + 1) * SUBLANES]
        if shift:
            piece = pltpu.roll(piece, shift, 0)
        if out is None:
            out = piece
        else:
            m = functools.reduce(lambda p, q: p | q, [sub == r for r in rs])
            out = jnp.where(m, piece, out)
    return out


def _row_range_sum(x, lo, hi):
    if hi - lo == 1:
        return x[lo:lo + 1]
    sub = lax.broadcasted_iota(jnp.int32, (SUBLANES, x.shape[1]), 0)
    tot = None
    for sl in range(lo // SUBLANES, (hi - 1) // SUBLANES + 1):
        piece = x[sl * SUBLANES:(sl + 1) * SUBLANES]
        r0 = max(lo, sl * SUBLANES) - sl * SUBLANES
        r1 = min(hi, (sl + 1) * SUBLANES) - sl * SUBLANES
        if (r0, r1) != (0, SUBLANES):
            piece = jnp.where((sub >= r0) & (sub < r1), piece, 0.0)
        tot = piece if tot is None else tot + piece
    return jnp.sum(tot, axis=0, keepdims=True)


def _topk_blind(s, k):
    orig = s
    rank = jnp.full(s.shape, float(k), F32)
    vals = []
    for j in range(k):
        m = jnp.max(s, axis=0, keepdims=True)
        hit = s == m
        rank = jnp.where(hit, float(j), rank)
        s = jnp.where(hit, -jnp.inf, s)
        vals.append(m)
    removed = jnp.sum(jnp.where(s != orig, 1.0, 0.0), axis=0, keepdims=True)
    return vals, rank, jnp.abs(removed - float(k))


def _route_head(s1, s2, k, exact):
    if exact:
        v1, rank1 = _topk_rows(s1, k)
        v2, rank2 = _topk_rows(s2, k)
        ties = jnp.zeros((1, s1.shape[1]), F32)
    else:
        v1, rank1, t1 = _topk_blind(s1, k)
        v2, rank2, t2 = _topk_blind(s2, k)
        ties = t1 + t2
    v1m = jnp.concatenate(v1, axis=0)
    v2m = jnp.concatenate(v2, axis=0)
    pairs = _cand_pairs(k)
    nslab = -(-len(pairs) // SUBLANES)
    padded = pairs + [None] * (nslab * SUBLANES - len(pairs))
    slabs = []
    for sl in range(nslab):
        chunk = padded[sl * SUBLANES:(sl + 1) * SUBLANES]
        slabs.append(_gather_rows(v1m, [None if pr is None else pr[0] for pr in chunk])
                     + _gather_rows(v2m, [None if pr is None else pr[1] for pr in chunk]))
    cand = jnp.concatenate(slabs, axis=0)
    rows = lax.broadcasted_iota(jnp.int32, cand.shape, 0).astype(F32)
    cand = jnp.where(rows < float(len(pairs)), cand, -jnp.inf)
    c = cand
    for _ in range(k):
        m = jnp.max(c, axis=0, keepdims=True)
        if exact:
            idx = jnp.min(jnp.where(c == m, rows, float(cand.shape[0])), axis=0, keepdims=True)
            c = jnp.where(rows == idx, -jnp.inf, c)
        else:
            c = jnp.where(c == m, -jnp.inf, c)
    sel = c != cand
    self32 = sel.astype(F32)
    if not exact:
        ties = ties + jnp.abs(jnp.sum(self32, axis=0, keepdims=True) - float(k))
    cmax = v1[0] + v2[0]
    zsum = jnp.sum(jnp.where(sel, jnp.exp(cand - cmax), 0.0), axis=0, keepdims=True)
    cn = jnp.zeros(s1.shape, F32)
    for a in range(k):
        lo = pairs.index((a, 0))
        hi = lo + k // (a + 1)
        cn = cn + jnp.where(rank1 == float(a), _row_range_sum(self32, lo, hi), 0.0)
    return cn, jnp.exp(s1 - v1[0]) / zsum, rank2, jnp.exp(s2 - v2[0]), ties


def _merge_route_kernel(x_ref, ya_ref, yb_ref, g_ref, wo_ref, n2_ref, wq_ref, keys_ref,
                        x1_ref, xn_ref, rta_ref, rtb_ref, *, nheads, nkeys, dhalf):
    d = x_ref.shape[1]
    g = _sigmoid(g_ref[...])
    mix = g[:, :d] * ya_ref[...] + g[:, d:] * yb_ref[...]
    x1 = x_ref[...] + _dot(mix.astype(BF16), wo_ref[...])
    x1_ref[...] = x1
    ms = jnp.mean(x1 * x1, axis=-1, keepdims=True)
    xn = (x1 * lax.rsqrt(ms + EPS) * n2_ref[...]).astype(BF16)
    xn_ref[...] = xn
    q = _dot(xn, wq_ref[...]).astype(BF16)

    k = PEER_TOPK
    for h in range(nheads):
        s1 = _dot_nt(keys_ref[h, 0], q[:, (2 * h) * dhalf:(2 * h + 1) * dhalf])
        s2 = _dot_nt(keys_ref[h, 1], q[:, (2 * h + 1) * dhalf:(2 * h + 2) * dhalf])

        def emit(exact, h=h, s1=s1, s2=s2):
            cn, e1, rank2, e2, ties = _route_head(s1, s2, k, exact)
            rta_ref[h, 0] = cn
            rta_ref[h, 1] = e1
            rtb_ref[h, 0] = rank2.astype(BF16)
            rtb_ref[h, 1] = e2.astype(BF16)
            return ties

        ties = emit(False)

        @pl.when(jnp.max(ties) > 0.0)
        def _():
            emit(True)


def _merge_route(x, ya, yb, proj, p, cols, tm):
    n, d = x.shape
    nheads, _, nkeys, dhalf = p["keys"].shape
    dq = p["wq"].shape[1]
    const2 = lambda i: (0, 0)
    kern = functools.partial(_merge_route_kernel, nheads=nheads, nkeys=nkeys, dhalf=dhalf)
    return pl.pallas_call(
        kern,
        grid=(n // tm,),
        in_specs=[
            pl.BlockSpec((tm, d), lambda i: (i, 0)),
            pl.BlockSpec((tm, d), lambda i: (i, 0)),
            pl.BlockSpec((tm, d), lambda i: (i, 0)),
            pl.BlockSpec((tm, 2 * d), lambda i: (i, cols["gate"] // (2 * d))),
            pl.BlockSpec((d, d), const2),
            pl.BlockSpec((1, d), const2),
            pl.BlockSpec((d, dq), const2),
            pl.BlockSpec((nheads, 2, nkeys, dhalf), lambda i: (0, 0, 0, 0)),
        ],
        out_specs=[
            pl.BlockSpec((tm, d), lambda i: (i, 0)),
            pl.BlockSpec((tm, d), lambda i: (i, 0)),
            pl.BlockSpec((nheads, 2, nkeys, tm), lambda i: (0, 0, 0, i)),
            pl.BlockSpec((nheads, 2, nkeys, tm), lambda i: (0, 0, 0, i)),
        ],
        out_shape=[
            jax.ShapeDtypeStruct((n, d), F32),
            jax.ShapeDtypeStruct((n, d), BF16),
            jax.ShapeDtypeStruct((nheads, 2, nkeys, n), F32),
            jax.ShapeDtypeStruct((nheads, 2, nkeys, n), BF16),
        ],
        compiler_params=_cparams(("parallel",)),
        name="merge_route",
    )(x, ya, yb, proj, p["wo"], p["n2"], p["wq"], p["keys"])


def _peer_kernel(xn_ref, rta_ref, rtb_ref, u_ref, vt_ref, x1_ref, fw_ref, *rest,
                 nheads, nkeys, nblk, final_norm, split_tile):
    if split_tile is None:
        o_ref, o2_ref = rest[0], None
        acc_ref, ht0, ht1, wt0, wt1 = rest[1:]
    else:
        o_ref, o2_ref = rest[0], rest[1]
        acc_ref, ht0, ht1, wt0, wt1 = rest[2:]
    f = pl.program_id(0)
    eb = u_ref.shape[0]
    na = eb // nkeys
    e_b = lax.rem(jnp.maximum(f - 1, 0), nblk)
    e_c = lax.rem(jnp.maximum(f - 2, 0), nblk)

    @pl.when(f == 0)
    def _():
        for r in (ht0, ht1, wt0, wt1):
            r[...] = jnp.zeros_like(r)

    @pl.when(e_c == 0)
    def _():
        acc_ref[...] = jnp.zeros_like(acc_ref)

    def stages(ht_w, ht_r, wt_w, wt_r):
        ht_w[...] = _dot_nt(u_ref[...], xn_ref[...])
        a0 = pl.multiple_of(e_b * na, na)
        slabs = [(rta_ref[h, 0, pl.ds(a0, na), :], rta_ref[h, 1, pl.ds(a0, na), :])
                 for h in range(nheads)]
        tt = xn_ref.shape[0]
        nv = nkeys // BF16_ROWS
        for al in range(na):
            gates = [None] * nv
            for h in range(nheads):
                cn = jnp.broadcast_to(slabs[h][0][al:al + 1, :], (BF16_ROWS, tt)).astype(BF16)
                e1 = jnp.broadcast_to(slabs[h][1][al:al + 1, :], (BF16_ROWS, tt)).astype(BF16)
                for v in range(nv):
                    rk = rtb_ref[h, 0, v * BF16_ROWS:(v + 1) * BF16_ROWS, :]
                    p2 = rtb_ref[h, 1, v * BF16_ROWS:(v + 1) * BF16_ROWS, :]
                    t = jnp.where(rk < cn, p2, jnp.zeros_like(p2)) * e1
                    gates[v] = t if gates[v] is None else gates[v] + t
            for v in range(nv):
                r0 = al * nkeys + v * BF16_ROWS
                hv = ht_r[r0:r0 + BF16_ROWS, :]
                act = 0.5 * hv * (1.0 + lax.erf(hv * (1.0 / math.sqrt(2.0))))
                wt_w[r0:r0 + BF16_ROWS, :] = (gates[v].astype(F32) * act).astype(BF16)
        acc_ref[...] += _dot(vt_ref[...], wt_r[...])

    even = lax.rem(f, 2) == 0

    @pl.when(even)
    def _():
        stages(ht0, ht1, wt1, wt0)

    @pl.when(jnp.logical_not(even))
    def _():
        stages(ht1, ht0, wt0, wt1)

    @pl.when((f >= 2) & (e_c == nblk - 1))
    def _():
        x2 = x1_ref[...] + acc_ref[...].T
        if final_norm:
            ms = jnp.mean(x2 * x2, axis=-1, keepdims=True)
            x2 = x2 * lax.rsqrt(ms + EPS) * fw_ref[...]
        if split_tile is None:
            o_ref[...] = x2
        else:
            tile = (f - 2) // nblk

            @pl.when(tile < split_tile)
            def _():
                o_ref[...] = x2

            @pl.when(tile >= split_tile)
            def _():
                o2_ref[...] = x2


def _peer(xn, rta, rtb, u_bf, vt_bf, x1, fw, tt, eb, final_norm, split_rows=None):
    n, d = x1.shape
    ne = u_bf.shape[0]
    nheads, _, nkeys, _ = rta.shape
    nblk = ne // eb
    nflat = (n // tt) * nblk
    fa = lambda f: jnp.minimum(f, nflat - 1)
    fb = lambda f: jnp.clip(f - 1, 0, nflat - 1)
    fc = lambda f: jnp.clip(f - 2, 0, nflat - 1)
    tile_c = lambda f: fc(f) // nblk
    if split_rows is None:
        split_tile = None
        out_specs = pl.BlockSpec((tt, d), lambda f: (tile_c(f), 0))
        out_shape = jax.ShapeDtypeStruct((n, d), F32)
    else:
        split_tile = split_rows // tt
        out_specs = [
            pl.BlockSpec((tt, d), lambda f: (jnp.minimum(tile_c(f), split_tile - 1), 0)),
            pl.BlockSpec((tt, d), lambda f: (jnp.maximum(tile_c(f) - split_tile, 0), 0)),
        ]
        out_shape = [jax.ShapeDtypeStruct((split_rows, d), F32),
                     jax.ShapeDtypeStruct((n - split_rows, d), F32)]
    kern = functools.partial(_peer_kernel, nheads=nheads, nkeys=nkeys, nblk=nblk,
                             final_norm=final_norm, split_tile=split_tile)
    return pl.pallas_call(
        kern,
        grid=(nflat + 2,),
        in_specs=[
            pl.BlockSpec((tt, d), lambda f: (fa(f) // nblk, 0)),
            pl.BlockSpec((nheads, 2, nkeys, tt), lambda f: (0, 0, 0, fb(f) // nblk)),
            pl.BlockSpec((nheads, 2, nkeys, tt), lambda f: (0, 0, 0, fb(f) // nblk)),
            pl.BlockSpec((eb, d), lambda f: (fa(f) % nblk, 0)),
            pl.BlockSpec((d, eb), lambda f: (0, fc(f) % nblk)),
            pl.BlockSpec((tt, d), lambda f: (fc(f) // nblk, 0)),
            pl.BlockSpec((1, d), lambda f: (0, 0)),
        ],
        out_specs=out_specs,
        out_shape=out_shape,
        scratch_shapes=[
            pltpu.VMEM((d, tt), F32),
            pltpu.VMEM((eb, tt), F32), pltpu.VMEM((eb, tt), F32),
            pltpu.VMEM((eb, tt), BF16), pltpu.VMEM((eb, tt), BF16),
        ],
        compiler_params=_cparams(("arbitrary",)),
        name="peer_experts",
    )(xn, rta, rtb, u_bf, vt_bf, x1, fw)


def _pad_rows(a, n):
    return jnp.pad(a, ((0, n - a.shape[0]),) + ((0, 0),) * (a.ndim - 1))


def kernel(x_prompt, x_sample, cache_conf, cache_mconv, state_ssm, meta_tokens, norm1_w, w_in, b_in, conf_dw_w, conf_dw_b, conf_ln_g, conf_ln_b, w_conf_out, b_conf_out, m_conv_w, m_conv_b, dt_bias, A_log, D_skip, m_norm_w, w_m_out, w_o, norm2_w, peer_w_q, peer_keys, peer_u, peer_v, final_norm_w):
    bp, sp, d = x_prompt.shape
    db, ds, _ = x_sample.shape
    depth = norm1_w.shape[0]
    c_conv = conf_dw_w.shape[2]
    nheads = dt_bias.shape[1]
    di = nheads * M_HEADDIM
    gn = M_GROUPS * M_STATE
    cd = di + 2 * gn
    n_p, n_s = bp * sp, db * ds
    n_all = -(-(n_p + n_s + META_ROWS) // ROW_ALIGN) * ROW_ALIGN
    off_s, off_m = n_p, n_p + n_s

    cols = {"glu_a": 0, "glu_b": c_conv, "z": 2 * c_conv, "xbc": 2 * c_conv + di}
    assert cols["z"] % di == 0 and cols["xbc"] % cd == 0
    cols["gate"] = cols["xbc"] + cd
    assert cols["gate"] % (2 * d) == 0
    cols["dt"] = cols["gate"] + 2 * d
    ncol = cols["dt"] + LANES
    src_dt = 2 * c_conv + di + cd
    src_gate = src_dt + nheads
    tn = ncol // 9 if (ncol % 9 == 0 and (ncol // 9) % LANES == 0) else LANES
    lin_tm = LIN_ROWS if n_all % LIN_ROWS == 0 else ROW_ALIGN

    seq_nb = 8
    prompt_ls = 128
    assert sp % prompt_ls == 0 and db % seq_nb == 0 and n_p % LANES == 0 and n_s % LANES == 0

    x_all = jnp.concatenate([
        x_prompt.reshape(n_p, d), x_sample.reshape(n_s, d), meta_tokens.astype(F32),
        jnp.zeros((n_all - n_p - n_s - N_META, d), F32)], axis=0)

    n_mblk = (n_all - off_m) // META_ROWS
    zc = jnp.zeros((1, CONF_HIST, c_conv), F32)
    zm = jnp.zeros((1, MCONV_HIST, cd), F32)
    zh = jnp.zeros((1, 1, nheads, M_HEADDIM, M_STATE), F32)
    peer_tt = PEER_ROWS if n_all % PEER_ROWS == 0 else ROUTE_ROWS
    ssm_p = ssm_s = None
    outs = {k: [] for k in ("conf_p", "mconv_p", "conf_s", "mconv_s")}
    for l in range(depth):
        wl, bl = w_in[l], b_in[l]
        w_re = jnp.concatenate([wl[:, :src_dt], wl[:, src_gate:], wl[:, src_dt:src_gate],
                                jnp.zeros((d, LANES - nheads), F32)], axis=1).astype(BF16)
        b_re = jnp.concatenate([bl[:src_dt], bl[src_gate:], bl[src_dt:src_gate],
                                jnp.zeros((LANES - nheads,), F32)])[None, :]
        proj = _norm_linear(x_all, norm1_w[l][None, :], w_re, b_re, lin_tm, tn)

        pc = {
            "dww": _pad_rows(conf_dw_w[l], CONF_HIST), "dwb": conf_dw_b[l][None, :],
            "lng": conf_ln_g[l][None, :], "lnb": conf_ln_b[l][None, :],
            "wout": w_conf_out[l].astype(BF16), "bout": b_conf_out[l][None, :],
        }
        ps = {
            "mcw": _pad_rows(m_conv_w[l], MCONV_HIST), "mcb": m_conv_b[l][None, :],
            "dtb": jnp.pad(dt_bias[l], (0, LANES - nheads))[None, :],
            "alog": jnp.pad(A_log[l], (0, LANES - nheads))[None, :],
            "dskip": jnp.repeat(D_skip[l], M_HEADDIM)[None, :],
            "mnw": m_norm_w[l][None, :], "wmout": w_m_out[l].astype(BF16),
        }

        ya, cf_m = _conf_branch(proj, zc, pc, None, row_off=off_m, nseq=n_mblk, nb=1, ls=META_ROWS,
                                lv=N_META, n_tiles=1, cache_bcast=True)
        yb, h_m, mc_m = _ssd_branch(proj, zm, zh, ps, cols, None, None, h0_layer=0, out_layer=0,
                                    out_depth=1, row_off=off_m, nseq=n_mblk, nb=1, ls=META_ROWS,
                                    lv=N_META, n_chunks=1, state_bcast=True)
        ya, cf_p = _conf_branch(proj, cf_m, pc, ya, row_off=0, nseq=bp, nb=1, ls=prompt_ls,
                                lv=prompt_ls, n_tiles=sp // prompt_ls, cache_bcast=True)
        yb, ssm_p, mc_p = _ssd_branch(proj, mc_m, h_m, ps, cols, yb, ssm_p, h0_layer=0, out_layer=l,
                                      out_depth=depth, row_off=0, nseq=bp, nb=1, ls=prompt_ls,
                                      lv=prompt_ls, n_chunks=sp // prompt_ls, state_bcast=True)
        cf_in = jnp.pad(cache_conf[l], ((0, 0), (CONF_HIST - (CONF_W - 1), 0), (0, 0)))
        mc_in = jnp.pad(cache_mconv[l], ((0, 0), (MCONV_HIST - (M_CONV_W - 1), 0), (0, 0)))
        ya, cf_s = _conf_branch(proj, cf_in, pc, ya, row_off=off_s, nseq=db, nb=seq_nb, ls=ds, lv=ds,
                                n_tiles=1, cache_bcast=False)
        yb, ssm_s, mc_s = _ssd_branch(proj, mc_in, state_ssm, ps, cols, yb, ssm_s, h0_layer=l,
                                      out_layer=l, out_depth=depth, row_off=off_s, nseq=db,
                                      nb=seq_nb, ls=ds, lv=ds, n_chunks=1, state_bcast=False)

        pm = {"wo": w_o[l].astype(BF16), "n2": norm2_w[l][None, :], "wq": peer_w_q[l].astype(BF16),
              "keys": peer_keys[l].astype(BF16)}
        x1, xn2, rta, rtb = _merge_route(x_all, ya, yb, proj, pm, cols, ROUTE_ROWS)
        last = l == depth - 1
        x_all = _peer(xn2, rta, rtb, peer_u[l].astype(BF16), peer_v[l].T.astype(BF16), x1,
                      final_norm_w[None, :], peer_tt, PEER_KEY_ROWS * peer_keys.shape[3], last,
                      split_rows=n_p if (last and n_p % peer_tt == 0) else None)

        outs["conf_p"].append(cf_p[:, CONF_HIST - (CONF_W - 1):])
        outs["mconv_p"].append(mc_p[:, MCONV_HIST - (M_CONV_W - 1):])
        outs["conf_s"].append(cf_s[:, CONF_HIST - (CONF_W - 1):])
        outs["mconv_s"].append(mc_s[:, MCONV_HIST - (M_CONV_W - 1):])

    if isinstance(x_all, (list, tuple)):
        y_prompt = x_all[0].reshape(bp, sp, d)
        y_sample = x_all[1][:n_s].reshape(db, ds, d)
    else:
        y_prompt = x_all[:n_p].reshape(bp, sp, d)
        y_sample = x_all[off_s:off_s + n_s].reshape(db, ds, d)
    return (y_prompt, y_sample,
            jnp.stack(outs["conf_p"]), jnp.stack(outs["mconv_p"]), ssm_p,
            jnp.stack(outs["conf_s"]), jnp.stack(outs["mconv_s"]), ssm_s)
```

```python
import functools
import math

import jax
import jax.numpy as jnp
from jax import lax
from jax.experimental import pallas as pl
from jax.experimental.pallas import tpu as pltpu

F32 = jnp.float32
BF16 = jnp.bfloat16

EPS = 1e-6
N_META = 16
CONF_W = 31
M_CONV_W = 4
M_HEADDIM = 64
M_GROUPS = 8
M_STATE = 128
PEER_TOPK = 16
LANES = 128
SUBLANES = 8
BF16_ROWS = 16
CONF_HIST = 32
MCONV_HIST = 8
META_ROWS = 128
ROW_ALIGN = 512
LIN_ROWS = 1280
ROUTE_ROWS = 256
PEER_ROWS = 512
PEER_KEY_ROWS = 8
VMEM_LIMIT = 56 * 1024 * 1024


def _cparams(sem):
    return pltpu.CompilerParams(dimension_semantics=sem, vmem_limit_bytes=VMEM_LIMIT)


def _split3(a):
    p1 = a.astype(BF16)
    r1 = a - p1.astype(F32)
    p2 = r1.astype(BF16)
    r2 = r1 - p2.astype(F32)
    p3 = r2.astype(BF16)
    return p1, p2, p3


def _dot(a, b):
    return jnp.dot(a, b, preferred_element_type=F32)


def _dot_nt(a, b):
    return lax.dot_general(a, b, (((1,), (1,)), ((), ())), preferred_element_type=F32)


def _dot_tn(a, b):
    return lax.dot_general(a, b, (((0,), (0,)), ((), ())), preferred_element_type=F32)


def _sel_right(m01, a):
    p1, p2, p3 = _split3(a)
    return _dot(m01, p1) + _dot(m01, p2) + _dot(m01, p3)


def _sel_left(a, m01):
    p1, p2, p3 = _split3(a)
    return _dot(p1, m01) + _dot(p2, m01) + _dot(p3, m01)


def _transpose_id(a, n):
    eye = (lax.broadcasted_iota(jnp.int32, (n, n), 0)
           == lax.broadcasted_iota(jnp.int32, (n, n), 1)).astype(BF16)
    p1, p2, p3 = _split3(a)
    return _dot_nt(eye, p1) + _dot_nt(eye, p2) + _dot_nt(eye, p3)


def _sigmoid(x):
    return 1.0 / (1.0 + jnp.exp(-x))


def _lin_kernel(x_ref, nw_ref, w_ref, b_ref, o_ref, xn_ref):
    @pl.when(pl.program_id(1) == 0)
    def _():
        x = x_ref[...]
        ms = jnp.mean(x * x, axis=-1, keepdims=True)
        xn_ref[...] = (x * lax.rsqrt(ms + EPS) * nw_ref[...]).astype(BF16)

    o_ref[...] = _dot(xn_ref[...], w_ref[...]) + b_ref[...]


def _norm_linear(x, nw, w, b, tm, tn):
    n, d = x.shape
    nc = w.shape[1]
    return pl.pallas_call(
        _lin_kernel,
        grid=(n // tm, nc // tn),
        in_specs=[
            pl.BlockSpec((tm, d), lambda i, j: (i, 0)),
            pl.BlockSpec((1, d), lambda i, j: (0, 0)),
            pl.BlockSpec((d, tn), lambda i, j: (0, j)),
            pl.BlockSpec((1, tn), lambda i, j: (0, j)),
        ],
        out_specs=pl.BlockSpec((tm, tn), lambda i, j: (i, j)),
        out_shape=jax.ShapeDtypeStruct((n, nc), F32),
        scratch_shapes=[pltpu.VMEM((tm, d), BF16)],
        compiler_params=_cparams(("parallel", "arbitrary")),
        name="norm_linear",
    )(x, nw, w, b)


def _conf_kernel(ga_ref, gb_ref, cache_ref, dww_ref, dwb_ref, lng_ref, lnb_ref, wout_ref, bout_ref,
                 ya_ref, hist_ref, win_ref, c_scr, *, nb, ls, lv):
    @pl.when(pl.program_id(1) == 0)
    def _():
        win_ref[:, 0:CONF_HIST, :] = cache_ref[...]

    u = ga_ref[...] * _sigmoid(gb_ref[...])
    for j in range(nb):
        win_ref[j, CONF_HIST:CONF_HIST + ls, :] = u[j * ls:(j + 1) * ls, :]

    off = CONF_HIST - (CONF_W - 1)
    for j in range(nb):
        for cb in range(dwb_ref.shape[1] // LANES):
            cs = slice(cb * LANES, (cb + 1) * LANES)
            acc = jnp.broadcast_to(dwb_ref[:, cs], (ls, LANES))
            for ph in range(SUBLANES):
                taps = [k for k in range(CONF_W) if (k + off) % SUBLANES == ph]
                if not taps:
                    continue
                span = max((k + off) // SUBLANES for k in taps) * SUBLANES + ls
                if ph:
                    view = pltpu.roll(win_ref[j, 0:span + SUBLANES, cs], span + SUBLANES - ph, 0)
                for k in taps:
                    m = (k + off) // SUBLANES * SUBLANES
                    rows = view[m:m + ls, :] if ph else win_ref[j, m:m + ls, cs]
                    acc = acc + rows * dww_ref[k:k + 1, cs]
            c_scr[j * ls:(j + 1) * ls, cs] = acc
    c = c_scr[...]

    mu = jnp.mean(c, axis=-1, keepdims=True)
    xc = c - mu
    y = xc * lax.rsqrt(jnp.mean(xc * xc, axis=-1, keepdims=True) + EPS)
    y = y * lng_ref[...] + lnb_ref[...]
    y = y * _sigmoid(y)
    ya_ref[...] = _dot(y.astype(BF16), wout_ref[...]) + bout_ref[...]

    for j in range(nb):
        tail = win_ref[j, lv:lv + CONF_HIST, :]
        win_ref[j, 0:CONF_HIST, :] = tail
    hist_ref[...] = win_ref[:, 0:CONF_HIST, :]


def _skip_ref(kern, pos):
    def wrapped(*refs):
        return kern(*refs[:pos], *refs[pos + 1:])
    return wrapped


def _conf_branch(proj, cache, p, y_prev, *, row_off, nseq, nb, ls, lv, n_tiles, cache_bcast):
    c = p["dww"].shape[1]
    d = p["wout"].shape[1]
    r = nb * ls
    bsteps = nseq // nb
    blk0 = row_off // r
    row_map = lambda b, i: (blk0 + b * n_tiles + i, 0)
    cache_map = (lambda b, i: (0, 0, 0)) if cache_bcast else (lambda b, i: (b, 0, 0))
    const2 = lambda b, i: (0, 0)
    kern = functools.partial(_conf_kernel, nb=nb, ls=ls, lv=lv)
    in_specs = [
        pl.BlockSpec((r, c), row_map),
        pl.BlockSpec((r, c), lambda b, i: (blk0 + b * n_tiles + i, 1)),
        pl.BlockSpec((nb, CONF_HIST, c), cache_map),
        pl.BlockSpec((CONF_HIST, c), const2),
        pl.BlockSpec((1, c), const2),
        pl.BlockSpec((1, c), const2),
        pl.BlockSpec((1, c), const2),
        pl.BlockSpec((c, d), const2),
        pl.BlockSpec((1, d), const2),
    ]
    args = [proj, proj, cache, p["dww"], p["dwb"], p["lng"], p["lnb"], p["wout"], p["bout"]]
    aliases = {}
    if y_prev is not None:
        aliases = {len(args): 0}
        kern = _skip_ref(kern, len(args))
        in_specs.append(pl.BlockSpec(memory_space=pl.ANY))
        args.append(y_prev)
    ya, hist = pl.pallas_call(
        kern,
        grid=(bsteps, n_tiles),
        in_specs=in_specs,
        out_specs=[
            pl.BlockSpec((r, d), row_map),
            pl.BlockSpec((nb, CONF_HIST, c), lambda b, i: (b, 0, 0)),
        ],
        out_shape=[
            jax.ShapeDtypeStruct((proj.shape[0], d), F32),
            jax.ShapeDtypeStruct((nseq, CONF_HIST, c), F32),
        ],
        scratch_shapes=[pltpu.VMEM((nb, CONF_HIST + ls, c), F32), pltpu.VMEM((r, c), F32)],
        input_output_aliases=aliases,
        compiler_params=_cparams(("parallel", "arbitrary")),
        name="conformer_branch",
    )(*args)
    return ya, hist


def _ssd_kernel(xbc_ref, z_ref, dt_ref, mc_ref, h0_ref, cw_ref, cb_ref, dtb_ref, alog_ref, dsk_ref,
                nw_ref, wout_ref, yb_ref, hout_ref, mcout_ref, cwin_scr, y_scr, xc_scr,
                *, nb, ls, lv, nheads, hdim, ngroups, nstate):
    r = nb * ls
    di = nheads * hdim
    gn = ngroups * nstate
    hpg = nheads // ngroups
    gw = hpg * hdim
    ls_shift = int(math.log2(ls))
    i = pl.program_id(1)

    @pl.when(i == 0)
    def _():
        hout_ref[...] = h0_ref[...]
        cwin_scr[:, 0:MCONV_HIST, :] = mc_ref[...]

    xbc = xbc_ref[...]
    for j in range(nb):
        cwin_scr[j, MCONV_HIST:MCONV_HIST + ls, :] = xbc[j * ls:(j + 1) * ls, :]
    off = MCONV_HIST - (M_CONV_W - 1)
    for j in range(nb):
        for cblk in range(cb_ref.shape[1] // LANES):
            cs = slice(cblk * LANES, (cblk + 1) * LANES)
            base = cwin_scr[j, :, cs]
            acc = jnp.broadcast_to(cb_ref[:, cs], (ls, LANES))
            for k in range(M_CONV_W):
                o = off + k
                if o % SUBLANES == 0:
                    rows = base[o:o + ls, :]
                else:
                    rows = pltpu.roll(base, MCONV_HIST + ls - o, 0)[0:ls, :]
                acc = acc + rows * cw_ref[k:k + 1, cs]
            xc_scr[j * ls:(j + 1) * ls, cs] = acc * _sigmoid(acc)
    xc = xc_scr[...]
    for j in range(nb):
        tail = cwin_scr[j, lv:lv + MCONV_HIST, :]
        cwin_scr[j, 0:MCONV_HIST, :] = tail
    mcout_ref[...] = cwin_scr[:, 0:MCONV_HIST, :]

    xs = xc[:, :di]
    bm = xc[:, di:di + gn].astype(BF16)
    cm = xc[:, di + gn:].astype(BF16)

    row1 = lax.broadcasted_iota(jnp.int32, (r, 1), 0)
    valid = (row1 & (ls - 1)) < lv
    dt = jax.nn.softplus(dt_ref[...] + dtb_ref[...])
    dt = jnp.where(valid, dt, 0.0)
    a = dt * (-jnp.exp(alog_ref[...]))

    ri = lax.broadcasted_iota(jnp.int32, (r, r), 0)
    ci = lax.broadcasted_iota(jnp.int32, (r, r), 1)
    same = (ri >> ls_shift) == (ci >> ls_shift)
    causal = same & (ci <= ri)
    acs = _sel_right(causal.astype(BF16), a)
    tot = _sel_right(same.astype(BF16), a)
    acs_t = _transpose_id(acs, LANES)
    etot_t = jnp.exp(_transpose_id(tot, LANES))

    expand = (lax.broadcasted_iota(jnp.int32, (LANES, di), 0)
              == (lax.broadcasted_iota(jnp.int32, (LANES, di), 1) >> int(math.log2(hdim)))).astype(BF16)
    dt_rep = _sel_left(dt, expand)
    eacs_rep = _sel_left(jnp.exp(acs), expand)
    wend_rep = _sel_left(jnp.exp(tot - acs) * dt, expand)
    dx = (xs * dt_rep).astype(BF16)
    wdx = xs * wend_rep

    lane = lax.broadcasted_iota(jnp.int32, (r, LANES), 1)
    heads_per_blk = LANES // hdim
    for g in range(ngroups):
        bg = bm[:, g * nstate:(g + 1) * nstate]
        cg = cm[:, g * nstate:(g + 1) * nstate]
        cb = _dot_nt(cg, bg)
        for blk in range(hpg // heads_per_blk):
            col0 = g * gw + blk * LANES
            dxb = dx[:, col0:col0 + LANES]
            yacc = None
            for q in range(heads_per_blk):
                h = g * hpg + blk * heads_per_blk + q
                seg = acs[:, h:h + 1] - acs_t[h:h + 1, :]
                dec = jnp.exp(jnp.where(causal, seg, -jnp.inf))
                m = (cb * dec).astype(BF16)
                hm = (lane >= q * hdim) & (lane < (q + 1) * hdim)
                t = _dot(m, jnp.where(hm, dxb, jnp.zeros_like(dxb)))
                yacc = t if yacc is None else yacc + t
            y_scr[:, col0:col0 + LANES] = yacc

    col_l = lax.broadcasted_iota(jnp.int32, (r, LANES), 0)

    def seq_body(j, carry):
        if nb == 1:
            wdx_j = wdx
            mrow = None
        else:
            mrow = (row1 >> ls_shift) == j
            wdx_j = jnp.where(mrow, wdx, 0.0)
        wdx_j = wdx_j.astype(BF16)
        selc = (col_l == (j + 1) * ls - 1).astype(BF16)
        el = _sel_left(etot_t, selc)
        for g in range(ngroups):
            bg = bm[:, g * nstate:(g + 1) * nstate]
            cg = cm[:, g * nstate:(g + 1) * nstate]
            hg = jnp.concatenate([hout_ref[j, g * hpg + q] for q in range(hpg)], axis=0)
            yoff = _dot_nt(cg, hg.astype(BF16)) * eacs_rep[:, g * gw:(g + 1) * gw]
            if mrow is not None:
                yoff = jnp.where(mrow, yoff, 0.0)
            y_scr[:, g * gw:(g + 1) * gw] += yoff
            st = _dot_tn(wdx_j[:, g * gw:(g + 1) * gw], bg)
            for q in range(hpg):
                h = g * hpg + q
                scale = jnp.broadcast_to(el[h:h + 1, :], (hdim, nstate))
                hout_ref[j, h] = scale * hout_ref[j, h] + st[q * hdim:(q + 1) * hdim, :]
        return carry

    if nb == 1:
        seq_body(0, 0)
    else:
        lax.fori_loop(0, nb, seq_body, 0)

    y = y_scr[...] + dsk_ref[...] * xs
    z = z_ref[...]
    y = y * (z * _sigmoid(z))
    outs = []
    gsz = di // M_GROUPS
    for g in range(M_GROUPS):
        blk = y[:, g * gsz:(g + 1) * gsz]
        outs.append(blk * lax.rsqrt(jnp.mean(blk * blk, axis=-1, keepdims=True) + EPS))
    yn = jnp.concatenate(outs, axis=1) * nw_ref[...]
    yb_ref[...] = _dot(yn.astype(BF16), wout_ref[...])


def _ssd_branch(proj, mcache, h0, p, cols, y_prev, hs_prev, *, h0_layer, out_layer, out_depth,
                row_off, nseq, nb, ls, lv, n_chunks, state_bcast):
    nheads, hdim, nstate = h0.shape[2], h0.shape[3], h0.shape[4]
    di = nheads * hdim
    cd = p["mcw"].shape[1]
    d = p["wmout"].shape[1]
    r = nb * ls
    bsteps = nseq // nb
    blk0 = row_off // r
    const2 = lambda b, i: (0, 0)
    st_map = ((lambda b, i: (h0_layer, 0, 0, 0, 0)) if state_bcast
              else (lambda b, i: (h0_layer, b, 0, 0, 0)))
    mc_map = (lambda b, i: (0, 0, 0)) if state_bcast else (lambda b, i: (b, 0, 0))
    kern = functools.partial(_ssd_kernel, nb=nb, ls=ls, lv=lv, nheads=nheads, hdim=hdim,
                             ngroups=M_GROUPS, nstate=nstate)
    in_specs = [
        pl.BlockSpec((r, cd), lambda b, i: (blk0 + b * n_chunks + i, cols["xbc"] // cd)),
        pl.BlockSpec((r, di), lambda b, i: (blk0 + b * n_chunks + i, cols["z"] // di)),
        pl.BlockSpec((r, LANES), lambda b, i: (blk0 + b * n_chunks + i, cols["dt"] // LANES)),
        pl.BlockSpec((nb, MCONV_HIST, cd), mc_map),
        pl.BlockSpec((None, nb, nheads, hdim, nstate), st_map),
        pl.BlockSpec((MCONV_HIST, cd), const2),
        pl.BlockSpec((1, cd), const2),
        pl.BlockSpec((1, LANES), const2),
        pl.BlockSpec((1, LANES), const2),
        pl.BlockSpec((1, di), const2),
        pl.BlockSpec((1, di), const2),
        pl.BlockSpec((di, d), const2),
    ]
    args = [proj, proj, proj, mcache, h0, p["mcw"], p["mcb"], p["dtb"], p["alog"], p["dskip"],
            p["mnw"], p["wmout"]]
    aliases = {}
    n_in = len(args)
    for prev, out_idx in ((y_prev, 0), (hs_prev, 1)):
        if prev is not None:
            aliases[len(args)] = out_idx
            in_specs.append(pl.BlockSpec(memory_space=pl.ANY))
            args.append(prev)
    for pos in range(n_in, len(args)):
        kern = _skip_ref(kern, pos)
    yb, hout, mcout = pl.pallas_call(
        kern,
        grid=(bsteps, n_chunks),
        in_specs=in_specs,
        out_specs=[
            pl.BlockSpec((r, d), lambda b, i: (blk0 + b * n_chunks + i, 0)),
            pl.BlockSpec((None, nb, nheads, hdim, nstate), lambda b, i: (out_layer, b, 0, 0, 0)),
            pl.BlockSpec((nb, MCONV_HIST, cd), lambda b, i: (b, 0, 0)),
        ],
        out_shape=[
            jax.ShapeDtypeStruct((proj.shape[0], d), F32),
            jax.ShapeDtypeStruct((out_depth, nseq, nheads, hdim, nstate), F32),
            jax.ShapeDtypeStruct((nseq, MCONV_HIST, cd), F32),
        ],
        scratch_shapes=[
            pltpu.VMEM((nb, MCONV_HIST + ls, cd), F32),
            pltpu.VMEM((r, di), F32),
            pltpu.VMEM((r, cd), F32),
        ],
        input_output_aliases=aliases,
        compiler_params=_cparams(("parallel", "arbitrary")),
        name="ssd_branch",
    )(*args)
    return yb, hout, mcout


def _topk_rows(s, k):
    n = s.shape[0]
    rows = lax.broadcasted_iota(jnp.int32, s.shape, 0).astype(F32)
    rank = jnp.full(s.shape, float(k), F32)
    vals = []
    for j in range(k):
        m = jnp.max(s, axis=0, keepdims=True)
        idx = jnp.min(jnp.where(s == m, rows, float(n)), axis=0, keepdims=True)
        hit = rows == idx
        rank = jnp.where(hit, float(j), rank)
        s = jnp.where(hit, -jnp.inf, s)
        vals.append(m)
    return vals, rank


def _cand_pairs(k):
    return [(a, j) for a in range(k) for j in range(k // (a + 1))]


def _gather_rows(src, idxs):
    sub = lax.broadcasted_iota(jnp.int32, (SUBLANES, src.shape[1]), 0)
    groups = {}
    for r, ix in enumerate(idxs):
        if ix is not None:
            groups.setdefault((ix // SUBLANES, (r - ix) % SUBLANES), []).append(r)
    out = None
    for (slab, shift), rs in groups.items():
        piece = src[slab * SUBLANES:(slab + 1) * SUBLANES]
        if shift:
            piece = pltpu.roll(piece, shift, 0)
        if out is None:
            out = piece
        else:
            m = functools.reduce(lambda p, q: p | q, [sub == r for r in rs])
            out = jnp.where(m, piece, out)
    return out


def _row_range_sum(x, lo, hi):
    if hi - lo == 1:
        return x[lo:lo + 1]
    sub = lax.broadcasted_iota(jnp.int32, (SUBLANES, x.shape[1]), 0)
    tot = None
    for sl in range(lo // SUBLANES, (hi - 1) // SUBLANES + 1):
        piece = x[sl * SUBLANES:(sl + 1) * SUBLANES]
        r0 = max(lo, sl * SUBLANES) - sl * SUBLANES
        r1 = min(hi, (sl + 1) * SUBLANES) - sl * SUBLANES
        if (r0, r1) != (0, SUBLANES):
            piece = jnp.where((sub >= r0) & (sub < r1), piece, 0.0)
        tot = piece if tot is None else tot + piece
    return jnp.sum(tot, axis=0, keepdims=True)


def _topk_blind(s, k):
    orig = s
    rank = jnp.full(s.shape, float(k), F32)
    vals = []
    for j in range(k):
        m = jnp.max(s, axis=0, keepdims=True)
        hit = s == m
        rank = jnp.where(hit, float(j), rank)
        s = jnp.where(hit, -jnp.inf, s)
        vals.append(m)
    removed = jnp.sum(jnp.where(s != orig, 1.0, 0.0), axis=0, keepdims=True)
    return vals, rank, jnp.abs(removed - float(k))


def _route_head(s1, s2, k, exact):
    if exact:
        v1, rank1 = _topk_rows(s1, k)
        v2, rank2 = _topk_rows(s2, k)
        ties = jnp.zeros((1, s1.shape[1]), F32)
    else:
        v1, rank1, t1 = _topk_blind(s1, k)
        v2, rank2, t2 = _topk_blind(s2, k)
        ties = t1 + t2
    v1m = jnp.concatenate(v1, axis=0)
    v2m = jnp.concatenate(v2, axis=0)
    pairs = _cand_pairs(k)
    nslab = -(-len(pairs) // SUBLANES)
    padded = pairs + [None] * (nslab * SUBLANES - len(pairs))
    slabs = []
    for sl in range(nslab):
        chunk = padded[sl * SUBLANES:(sl + 1) * SUBLANES]
        slabs.append(_gather_rows(v1m, [None if pr is None else pr[0] for pr in chunk])
                     + _gather_rows(v2m, [None if pr is None else pr[1] for pr in chunk]))
    cand = jnp.concatenate(slabs, axis=0)
    rows = lax.broadcasted_iota(jnp.int32, cand.shape, 0).astype(F32)
    cand = jnp.where(rows < float(len(pairs)), cand, -jnp.inf)
    c = cand
    for _ in range(k):
        m = jnp.max(c, axis=0, keepdims=True)
        if exact:
            idx = jnp.min(jnp.where(c == m, rows, float(cand.shape[0])), axis=0, keepdims=True)
            c = jnp.where(rows == idx, -jnp.inf, c)
        else:
            c = jnp.where(c == m, -jnp.inf, c)
    sel = c != cand
    self32 = sel.astype(F32)
    if not exact:
        ties = ties + jnp.abs(jnp.sum(self32, axis=0, keepdims=True) - float(k))
    cmax = v1[0] + v2[0]
    zsum = jnp.sum(jnp.where(sel, jnp.exp(cand - cmax), 0.0), axis=0, keepdims=True)
    cn = jnp.zeros(s1.shape, F32)
    for a in range(k):
        lo = pairs.index((a, 0))
        hi = lo + k // (a + 1)
        cn = cn + jnp.where(rank1 == float(a), _row_range_sum(self32, lo, hi), 0.0)
    return cn, 0.5 * jnp.exp(s1 - v1[0]) / zsum, rank2, jnp.exp(s2 - v2[0]), ties


def _merge_route_kernel(x_ref, ya_ref, yb_ref, g_ref, wo_ref, n2_ref, wq_ref, keys_ref,
                        x1_ref, xn_ref, rta_ref, rtb_ref, *, nheads, nkeys, dhalf):
    d = x_ref.shape[1]
    g = _sigmoid(g_ref[...])
    mix = g[:, :d] * ya_ref[...] + g[:, d:] * yb_ref[...]
    x1 = x_ref[...] + _dot(mix.astype(BF16), wo_ref[...])
    x1_ref[...] = x1
    ms = jnp.mean(x1 * x1, axis=-1, keepdims=True)
    xn = (x1 * lax.rsqrt(ms + EPS) * n2_ref[...]).astype(BF16)
    xn_ref[...] = xn
    q = _dot(xn, wq_ref[...]).astype(BF16)

    k = PEER_TOPK
    for h in range(nheads):
        s1 = _dot_nt(keys_ref[h, 0], q[:, (2 * h) * dhalf:(2 * h + 1) * dhalf])
        s2 = _dot_nt(keys_ref[h, 1], q[:, (2 * h + 1) * dhalf:(2 * h + 2) * dhalf])

        def emit(exact, h=h, s1=s1, s2=s2):
            ties = None
            for lg in range(s1.shape[1] // LANES):
                ls_ = slice(lg * LANES, (lg + 1) * LANES)
                cn, e1, rank2, e2, t = _route_head(s1[:, ls_], s2[:, ls_], k, exact)
                rta_ref[h, 0, :, ls_] = cn
                rta_ref[h, 1, :, ls_] = e1
                rtb_ref[h, 0, :, ls_] = rank2.astype(BF16)
                rtb_ref[h, 1, :, ls_] = e2.astype(BF16)
                ties = t if ties is None else jnp.maximum(ties, t)
            return ties

        ties = emit(False)

        @pl.when(jnp.max(ties) > 0.0)
        def _():
            emit(True)


def _merge_route(x, ya, yb, proj, p, cols, tm):
    n, d = x.shape
    nheads, _, nkeys, dhalf = p["keys"].shape
    dq = p["wq"].shape[1]
    const2 = lambda i: (0, 0)
    kern = functools.partial(_merge_route_kernel, nheads=nheads, nkeys=nkeys, dhalf=dhalf)
    return pl.pallas_call(
        kern,
        grid=(n // tm,),
        in_specs=[
            pl.BlockSpec((tm, d), lambda i: (i, 0)),
            pl.BlockSpec((tm, d), lambda i: (i, 0)),
            pl.BlockSpec((tm, d), lambda i: (i, 0)),
            pl.BlockSpec((tm, 2 * d), lambda i: (i, cols["gate"] // (2 * d))),
            pl.BlockSpec((d, d), const2),
            pl.BlockSpec((1, d), const2),
            pl.BlockSpec((d, dq), const2),
            pl.BlockSpec((nheads, 2, nkeys, dhalf), lambda i: (0, 0, 0, 0)),
        ],
        out_specs=[
            pl.BlockSpec((tm, d), lambda i: (i, 0)),
            pl.BlockSpec((tm, d), lambda i: (i, 0)),
            pl.BlockSpec((nheads, 2, nkeys, tm), lambda i: (0, 0, 0, i)),
            pl.BlockSpec((nheads, 2, nkeys, tm), lambda i: (0, 0, 0, i)),
        ],
        out_shape=[
            jax.ShapeDtypeStruct((n, d), F32),
            jax.ShapeDtypeStruct((n, d), BF16),
            jax.ShapeDtypeStruct((nheads, 2, nkeys, n), F32),
            jax.ShapeDtypeStruct((nheads, 2, nkeys, n), BF16),
        ],
        compiler_params=_cparams(("parallel",)),
        name="merge_route",
    )(x, ya, yb, proj, p["wo"], p["n2"], p["wq"], p["keys"])


def _peer_kernel(xn_ref, rta_ref, rtb_ref, u_ref, vt_ref, x1_ref, fw_ref, *rest,
                 nheads, nkeys, nblk, final_norm, split_tile):
    if split_tile is None:
        o_ref, o2_ref = rest[0], None
        acc_ref, ht0, ht1, wt0, wt1 = rest[1:]
    else:
        o_ref, o2_ref = rest[0], rest[1]
        acc_ref, ht0, ht1, wt0, wt1 = rest[2:]
    f = pl.program_id(0)
    eb = u_ref.shape[0]
    na = eb // nkeys
    e_b = lax.rem(jnp.maximum(f - 1, 0), nblk)
    e_c = lax.rem(jnp.maximum(f - 2, 0), nblk)

    @pl.when(f == 0)
    def _():
        for r in (ht0, ht1, wt0, wt1):
            r[...] = jnp.zeros_like(r)

    @pl.when(e_c == 0)
    def _():
        acc_ref[...] = jnp.zeros_like(acc_ref)

    def stages(ht_w, ht_r, wt_w, wt_r):
        ht_w[...] = _dot_nt(u_ref[...], xn_ref[...])
        a0 = pl.multiple_of(e_b * na, na)
        slabs = [(rta_ref[h, 0, pl.ds(a0, na), :], rta_ref[h, 1, pl.ds(a0, na), :])
                 for h in range(nheads)]
        tt = xn_ref.shape[0]
        nv = nkeys // BF16_ROWS
        for al in range(na):
            gates = [None] * nv
            for h in range(nheads):
                cn = jnp.broadcast_to(slabs[h][0][al:al + 1, :], (BF16_ROWS, tt)).astype(BF16)
                e1 = jnp.broadcast_to(slabs[h][1][al:al + 1, :], (BF16_ROWS, tt)).astype(BF16)
                for v in range(nv):
                    rk = rtb_ref[h, 0, v * BF16_ROWS:(v + 1) * BF16_ROWS, :]
                    p2 = rtb_ref[h, 1, v * BF16_ROWS:(v + 1) * BF16_ROWS, :]
                    t = jnp.where(rk < cn, p2, jnp.zeros_like(p2)) * e1
                    gates[v] = t if gates[v] is None else gates[v] + t
            for v in range(nv):
                r0 = al * nkeys + v * BF16_ROWS
                hv = ht_r[r0:r0 + BF16_ROWS, :]
                act = hv * (1.0 + lax.erf(hv * (1.0 / math.sqrt(2.0))))
                wt_w[r0:r0 + BF16_ROWS, :] = (gates[v].astype(F32) * act).astype(BF16)
        acc_ref[...] += _dot(vt_ref[...], wt_r[...])

    even = lax.rem(f, 2) == 0

    @pl.when(even)
    def _():
        stages(ht0, ht1, wt1, wt0)

    @pl.when(jnp.logical_not(even))
    def _():
        stages(ht1, ht0, wt0, wt1)

    @pl.when((f >= 2) & (e_c == nblk - 1))
    def _():
        x2 = x1_ref[...] + acc_ref[...].T
        if final_norm:
            ms = jnp.mean(x2 * x2, axis=-1, keepdims=True)
            x2 = x2 * lax.rsqrt(ms + EPS) * fw_ref[...]
        if split_tile is None:
            o_ref[...] = x2
        else:
            tile = (f - 2) // nblk

            @pl.when(tile < split_tile)
            def _():
                o_ref[...] = x2

            @pl.when(tile >= split_tile)
            def _():
                o2_ref[...] = x2


def _peer(xn, rta, rtb, u_bf, vt_bf, x1, fw, tt, eb, final_norm, split_rows=None):
    n, d = x1.shape
    ne = u_bf.shape[0]
    nheads, _, nkeys, _ = rta.shape
    nblk = ne // eb
    nflat = (n // tt) * nblk
    fa = lambda f: jnp.minimum(f, nflat - 1)
    fb = lambda f: jnp.clip(f - 1, 0, nflat - 1)
    fc = lambda f: jnp.clip(f - 2, 0, nflat - 1)
    tile_c = lambda f: fc(f) // nblk
    if split_rows is None:
        split_tile = None
        out_specs = pl.BlockSpec((tt, d), lambda f: (tile_c(f), 0))
        out_shape = jax.ShapeDtypeStruct((n, d), F32)
    else:
        split_tile = split_rows // tt
        out_specs = [
            pl.BlockSpec((tt, d), lambda f: (jnp.minimum(tile_c(f), split_tile - 1), 0)),
            pl.BlockSpec((tt, d), lambda f: (jnp.maximum(tile_c(f) - split_tile, 0), 0)),
        ]
        out_shape = [jax.ShapeDtypeStruct((split_rows, d), F32),
                     jax.ShapeDtypeStruct((n - split_rows, d), F32)]
    kern = functools.partial(_peer_kernel, nheads=nheads, nkeys=nkeys, nblk=nblk,
                             final_norm=final_norm, split_tile=split_tile)
    return pl.pallas_call(
        kern,
        grid=(nflat + 2,),
        in_specs=[
            pl.BlockSpec((tt, d), lambda f: (fa(f) // nblk, 0)),
            pl.BlockSpec((nheads, 2, nkeys, tt), lambda f: (0, 0, 0, fb(f) // nblk)),
            pl.BlockSpec((nheads, 2, nkeys, tt), lambda f: (0, 0, 0, fb(f) // nblk)),
            pl.BlockSpec((eb, d), lambda f: (fa(f) % nblk, 0)),
            pl.BlockSpec((d, eb), lambda f: (0, fc(f) % nblk)),
            pl.BlockSpec((tt, d), lambda f: (fc(f) // nblk, 0)),
            pl.BlockSpec((1, d), lambda f: (0, 0)),
        ],
        out_specs=out_specs,
        out_shape=out_shape,
        scratch_shapes=[
            pltpu.VMEM((d, tt), F32),
            pltpu.VMEM((eb, tt), F32), pltpu.VMEM((eb, tt), F32),
            pltpu.VMEM((eb, tt), BF16), pltpu.VMEM((eb, tt), BF16),
        ],
        compiler_params=_cparams(("arbitrary",)),
        name="peer_experts",
    )(xn, rta, rtb, u_bf, vt_bf, x1, fw)


def _pad_rows(a, n):
    return jnp.pad(a, ((0, n - a.shape[0]),) + ((0, 0),) * (a.ndim - 1))


def kernel(x_prompt, x_sample, cache_conf, cache_mconv, state_ssm, meta_tokens, norm1_w, w_in, b_in, conf_dw_w, conf_dw_b, conf_ln_g, conf_ln_b, w_conf_out, b_conf_out, m_conv_w, m_conv_b, dt_bias, A_log, D_skip, m_norm_w, w_m_out, w_o, norm2_w, peer_w_q, peer_keys, peer_u, peer_v, final_norm_w):
    bp, sp, d = x_prompt.shape
    db, ds, _ = x_sample.shape
    depth = norm1_w.shape[0]
    c_conv = conf_dw_w.shape[2]
    nheads = dt_bias.shape[1]
    di = nheads * M_HEADDIM
    gn = M_GROUPS * M_STATE
    cd = di + 2 * gn
    n_p, n_s = bp * sp, db * ds
    n_all = -(-(n_p + n_s + META_ROWS) // ROW_ALIGN) * ROW_ALIGN
    off_s, off_m = n_p, n_p + n_s

    cols = {"glu_a": 0, "glu_b": c_conv, "z": 2 * c_conv, "xbc": 2 * c_conv + di}
    assert cols["z"] % di == 0 and cols["xbc"] % cd == 0
    cols["gate"] = cols["xbc"] + cd
    assert cols["gate"] % (2 * d) == 0
    cols["dt"] = cols["gate"] + 2 * d
    ncol = cols["dt"] + LANES
    src_dt = 2 * c_conv + di + cd
    src_gate = src_dt + nheads
    tn = ncol // 9 if (ncol % 9 == 0 and (ncol // 9) % LANES == 0) else LANES
    lin_tm = LIN_ROWS if n_all % LIN_ROWS == 0 else ROW_ALIGN

    seq_nb = 8
    prompt_ls = 128
    assert sp % prompt_ls == 0 and db % seq_nb == 0 and n_p % LANES == 0 and n_s % LANES == 0

    x_all = jnp.concatenate([
        x_prompt.reshape(n_p, d), x_sample.reshape(n_s, d), meta_tokens.astype(F32),
        jnp.zeros((n_all - n_p - n_s - N_META, d), F32)], axis=0)

    n_mblk = (n_all - off_m) // META_ROWS
    zc = jnp.zeros((1, CONF_HIST, c_conv), F32)
    zm = jnp.zeros((1, MCONV_HIST, cd), F32)
    zh = jnp.zeros((1, 1, nheads, M_HEADDIM, M_STATE), F32)
    peer_tt = PEER_ROWS if n_all % PEER_ROWS == 0 else ROUTE_ROWS
    ssm_p = ssm_s = None
    outs = {k: [] for k in ("conf_p", "mconv_p", "conf_s", "mconv_s")}
    for l in range(depth):
        wl, bl = w_in[l], b_in[l]
        w_re = jnp.concatenate([wl[:, :src_dt], wl[:, src_gate:], wl[:, src_dt:src_gate],
                                jnp.zeros((d, LANES - nheads), F32)], axis=1).astype(BF16)
        b_re = jnp.concatenate([bl[:src_dt], bl[src_gate:], bl[src_dt:src_gate],
                                jnp.zeros((LANES - nheads,), F32)])[None, :]
        proj = _norm_linear(x_all, norm1_w[l][None, :], w_re, b_re, lin_tm, tn)

        pc = {
            "dww": _pad_rows(conf_dw_w[l], CONF_HIST), "dwb": conf_dw_b[l][None, :],
            "lng": conf_ln_g[l][None, :], "lnb": conf_ln_b[l][None, :],
            "wout": w_conf_out[l].astype(BF16), "bout": b_conf_out[l][None, :],
        }
        ps = {
            "mcw": _pad_rows(m_conv_w[l], MCONV_HIST), "mcb": m_conv_b[l][None, :],
            "dtb": jnp.pad(dt_bias[l], (0, LANES - nheads))[None, :],
            "alog": jnp.pad(A_log[l], (0, LANES - nheads))[None, :],
            "dskip": jnp.repeat(D_skip[l], M_HEADDIM)[None, :],
            "mnw": m_norm_w[l][None, :], "wmout": w_m_out[l].astype(BF16),
        }

        ya, cf_m = _conf_branch(proj, zc, pc, None, row_off=off_m, nseq=n_mblk, nb=1, ls=META_ROWS,
                                lv=N_META, n_tiles=1, cache_bcast=True)
        yb, h_m, mc_m = _ssd_branch(proj, zm, zh, ps, cols, None, None, h0_layer=0, out_layer=0,
                                    out_depth=1, row_off=off_m, nseq=n_mblk, nb=1, ls=META_ROWS,
                                    lv=N_META, n_chunks=1, state_bcast=True)
        ya, cf_p = _conf_branch(proj, cf_m, pc, ya, row_off=0, nseq=bp, nb=1, ls=prompt_ls,
                                lv=prompt_ls, n_tiles=sp // prompt_ls, cache_bcast=True)
        yb, ssm_p, mc_p = _ssd_branch(proj, mc_m, h_m, ps, cols, yb, ssm_p, h0_layer=0, out_layer=l,
                                      out_depth=depth, row_off=0, nseq=bp, nb=1, ls=prompt_ls,
                                      lv=prompt_ls, n_chunks=sp // prompt_ls, state_bcast=True)
        cf_in = jnp.pad(cache_conf[l], ((0, 0), (CONF_HIST - (CONF_W - 1), 0), (0, 0)))
        mc_in = jnp.pad(cache_mconv[l], ((0, 0), (MCONV_HIST - (M_CONV_W - 1), 0), (0, 0)))
        ya, cf_s = _conf_branch(proj, cf_in, pc, ya, row_off=off_s, nseq=db, nb=seq_nb, ls=ds, lv=ds,
                                n_tiles=1, cache_bcast=False)
        yb, ssm_s, mc_s = _ssd_branch(proj, mc_in, state_ssm, ps, cols, yb, ssm_s, h0_layer=l,
                                      out_layer=l, out_depth=depth, row_off=off_s, nseq=db,
                                      nb=seq_nb, ls=ds, lv=ds, n_chunks=1, state_bcast=False)

        pm = {"wo": w_o[l].astype(BF16), "n2": norm2_w[l][None, :], "wq": peer_w_q[l].astype(BF16),
              "keys": peer_keys[l].astype(BF16)}
        x1, xn2, rta, rtb = _merge_route(x_all, ya, yb, proj, pm, cols, ROUTE_ROWS)
        last = l == depth - 1
        x_all = _peer(xn2, rta, rtb, peer_u[l].astype(BF16), peer_v[l].T.astype(BF16), x1,
                      final_norm_w[None, :], peer_tt, PEER_KEY_ROWS * peer_keys.shape[3], last,
                      split_rows=n_p if (last and n_p % peer_tt == 0) else None)

        outs["conf_p"].append(cf_p[:, CONF_HIST - (CONF_W - 1):])
        outs["mconv_p"].append(mc_p[:, MCONV_HIST - (M_CONV_W - 1):])
        outs["conf_s"].append(cf_s[:, CONF_HIST - (CONF_W - 1):])
        outs["mconv_s"].append(mc_s[:, MCONV_HIST - (M_CONV_W - 1):])

    if isinstance(x_all, (list, tuple)):
        y_prompt = x_all[0].reshape(bp, sp, d)
        y_sample = x_all[1][:n_s].reshape(db, ds, d)
    else:
        y_prompt = x_all[:n_p].reshape(bp, sp, d)
        y_sample = x_all[off_s:off_s + n_s].reshape(db, ds, d)
    return (y_prompt, y_sample,
            jnp.stack(outs["conf_p"]), jnp.stack(outs["mconv_p"]), ssm_p,
            jnp.stack(outs["conf_s"]), jnp.stack(outs["mconv_s"]), ssm_s)
```

```python
import functools
import math

import jax
import jax.numpy as jnp
from jax import lax
from jax.experimental import pallas as pl
from jax.experimental.pallas import tpu as pltpu

F32 = jnp.float32
BF16 = jnp.bfloat16

EPS = 1e-6
N_META = 16
CONF_W = 31
M_CONV_W = 4
M_HEADDIM = 64
M_GROUPS = 8
M_STATE = 128
PEER_TOPK = 16
LANES = 128
SUBLANES = 8
BF16_ROWS = 16
CONF_HIST = 32
MCONV_HIST = 8
META_ROWS = 128
ROW_ALIGN = 512
LIN_ROWS = 1280
ROUTE_ROWS = 256
PEER_ROWS = 512
PEER_KEY_ROWS = 8
VMEM_LIMIT = 56 * 1024 * 1024


def _cparams(sem):
    return pltpu.CompilerParams(dimension_semantics=sem, vmem_limit_bytes=VMEM_LIMIT)


def _split3(a):
    p1 = a.astype(BF16)
    r1 = a - p1.astype(F32)
    p2 = r1.astype(BF16)
    r2 = r1 - p2.astype(F32)
    p3 = r2.astype(BF16)
    return p1, p2, p3


def _dot(a, b):
    return jnp.dot(a, b, preferred_element_type=F32)


def _dot_nt(a, b):
    return lax.dot_general(a, b, (((1,), (1,)), ((), ())), preferred_element_type=F32)


def _dot_tn(a, b):
    return lax.dot_general(a, b, (((0,), (0,)), ((), ())), preferred_element_type=F32)


def _sel_right(m01, a):
    p1, p2, p3 = _split3(a)
    return _dot(m01, p1) + _dot(m01, p2) + _dot(m01, p3)


def _sel_left(a, m01):
    p1, p2, p3 = _split3(a)
    return _dot(p1, m01) + _dot(p2, m01) + _dot(p3, m01)


def _transpose_id(a, n):
    eye = (lax.broadcasted_iota(jnp.int32, (n, n), 0)
           == lax.broadcasted_iota(jnp.int32, (n, n), 1)).astype(BF16)
    p1, p2, p3 = _split3(a)
    return _dot_nt(eye, p1) + _dot_nt(eye, p2) + _dot_nt(eye, p3)


def _sigmoid(x):
    return 1.0 / (1.0 + jnp.exp(-x))


def _lin_kernel(x_ref, nw_ref, w_ref, b_ref, o_ref, xn_ref):
    @pl.when(pl.program_id(1) == 0)
    def _():
        x = x_ref[...]
        ms = jnp.mean(x * x, axis=-1, keepdims=True)
        xn_ref[...] = (x * lax.rsqrt(ms + EPS) * nw_ref[...]).astype(BF16)

    o_ref[...] = _dot(xn_ref[...], w_ref[...]) + b_ref[...]


def _norm_linear(x, nw, w, b, tm, tn):
    n, d = x.shape
    nc = w.shape[1]
    return pl.pallas_call(
        _lin_kernel,
        grid=(n // tm, nc // tn),
        in_specs=[
            pl.BlockSpec((tm, d), lambda i, j: (i, 0)),
            pl.BlockSpec((1, d), lambda i, j: (0, 0)),
            pl.BlockSpec((d, tn), lambda i, j: (0, j)),
            pl.BlockSpec((1, tn), lambda i, j: (0, j)),
        ],
        out_specs=pl.BlockSpec((tm, tn), lambda i, j: (i, j)),
        out_shape=jax.ShapeDtypeStruct((n, nc), F32),
        scratch_shapes=[pltpu.VMEM((tm, d), BF16)],
        compiler_params=_cparams(("parallel", "arbitrary")),
        name="norm_linear",
    )(x, nw, w, b)


def _conf_kernel(ga_ref, gb_ref, cache_ref, dww_ref, dwb_ref, lng_ref, lnb_ref, wout_ref, bout_ref,
                 ya_ref, hist_ref, win_ref, c_scr, *, nb, ls, lv):
    @pl.when(pl.program_id(1) == 0)
    def _():
        win_ref[:, 0:CONF_HIST, :] = cache_ref[...]

    u = ga_ref[...] * _sigmoid(gb_ref[...])
    for j in range(nb):
        win_ref[j, CONF_HIST:CONF_HIST + ls, :] = u[j * ls:(j + 1) * ls, :]

    off = CONF_HIST - (CONF_W - 1)
    for j in range(nb):
        for cb in range(dwb_ref.shape[1] // LANES):
            cs = slice(cb * LANES, (cb + 1) * LANES)
            acc = jnp.broadcast_to(dwb_ref[:, cs], (ls, LANES))
            for ph in range(SUBLANES):
                taps = [k for k in range(CONF_W) if (k + off) % SUBLANES == ph]
                if not taps:
                    continue
                span = max((k + off) // SUBLANES for k in taps) * SUBLANES + ls
                if ph:
                    view = pltpu.roll(win_ref[j, 0:span + SUBLANES, cs], span + SUBLANES - ph, 0)
                for k in taps:
                    m = (k + off) // SUBLANES * SUBLANES
                    rows = view[m:m + ls, :] if ph else win_ref[j, m:m + ls, cs]
                    acc = acc + rows * dww_ref[k:k + 1, cs]
            c_scr[j * ls:(j + 1) * ls, cs] = acc
    c = c_scr[...]

    mu = jnp.mean(c, axis=-1, keepdims=True)
    xc = c - mu
    y = xc * lax.rsqrt(jnp.mean(xc * xc, axis=-1, keepdims=True) + EPS)
    y = y * lng_ref[...] + lnb_ref[...]
    y = y * _sigmoid(y)
    ya_ref[...] = _dot(y.astype(BF16), wout_ref[...]) + bout_ref[...]

    for j in range(nb):
        tail = win_ref[j, lv:lv + CONF_HIST, :]
        win_ref[j, 0:CONF_HIST, :] = tail
    hist_ref[...] = win_ref[:, 0:CONF_HIST, :]


def _skip_ref(kern, pos):
    def wrapped(*refs):
        return kern(*refs[:pos], *refs[pos + 1:])
    return wrapped


def _conf_branch(proj, cache, p, y_prev, *, row_off, nseq, nb, ls, lv, n_tiles, cache_bcast):
    c = p["dww"].shape[1]
    d = p["wout"].shape[1]
    r = nb * ls
    bsteps = nseq // nb
    blk0 = row_off // r
    row_map = lambda b, i: (blk0 + b * n_tiles + i, 0)
    cache_map = (lambda b, i: (0, 0, 0)) if cache_bcast else (lambda b, i: (b, 0, 0))
    const2 = lambda b, i: (0, 0)
    kern = functools.partial(_conf_kernel, nb=nb, ls=ls, lv=lv)
    in_specs = [
        pl.BlockSpec((r, c), row_map),
        pl.BlockSpec((r, c), lambda b, i: (blk0 + b * n_tiles + i, 1)),
        pl.BlockSpec((nb, CONF_HIST, c), cache_map),
        pl.BlockSpec((CONF_HIST, c), const2),
        pl.BlockSpec((1, c), const2),
        pl.BlockSpec((1, c), const2),
        pl.BlockSpec((1, c), const2),
        pl.BlockSpec((c, d), const2),
        pl.BlockSpec((1, d), const2),
    ]
    args = [proj, proj, cache, p["dww"], p["dwb"], p["lng"], p["lnb"], p["wout"], p["bout"]]
    aliases = {}
    if y_prev is not None:
        aliases = {len(args): 0}
        kern = _skip_ref(kern, len(args))
        in_specs.append(pl.BlockSpec(memory_space=pl.ANY))
        args.append(y_prev)
    ya, hist = pl.pallas_call(
        kern,
        grid=(bsteps, n_tiles),
        in_specs=in_specs,
        out_specs=[
            pl.BlockSpec((r, d), row_map),
            pl.BlockSpec((nb, CONF_HIST, c), lambda b, i: (b, 0, 0)),
        ],
        out_shape=[
            jax.ShapeDtypeStruct((proj.shape[0], d), F32),
            jax.ShapeDtypeStruct((nseq, CONF_HIST, c), F32),
        ],
        scratch_shapes=[pltpu.VMEM((nb, CONF_HIST + ls, c), F32), pltpu.VMEM((r, c), F32)],
        input_output_aliases=aliases,
        compiler_params=_cparams(("parallel", "arbitrary")),
        name="conformer_branch",
    )(*args)
    return ya, hist


def _ssd_kernel(xbc_ref, z_ref, dt_ref, mc_ref, h0_ref, cw_ref, cb_ref, dtb_ref, alog_ref, dsk_ref,
                nw_ref, wout_ref, yb_ref, hout_ref, mcout_ref, cwin_scr, y_scr, xc_scr,
                *, nb, ls, lv, nheads, hdim, ngroups, nstate):
    r = nb * ls
    di = nheads * hdim
    gn = ngroups * nstate
    hpg = nheads // ngroups
    gw = hpg * hdim
    ls_shift = int(math.log2(ls))
    i = pl.program_id(1)

    @pl.when(i == 0)
    def _():
        hout_ref[...] = h0_ref[...]
        cwin_scr[:, 0:MCONV_HIST, :] = mc_ref[...]

    xbc = xbc_ref[...]
    for j in range(nb):
        cwin_scr[j, MCONV_HIST:MCONV_HIST + ls, :] = xbc[j * ls:(j + 1) * ls, :]
    off = MCONV_HIST - (M_CONV_W - 1)
    for j in range(nb):
        for cblk in range(cb_ref.shape[1] // LANES):
            cs = slice(cblk * LANES, (cblk + 1) * LANES)
            base = cwin_scr[j, :, cs]
            acc = jnp.broadcast_to(cb_ref[:, cs], (ls, LANES))
            for k in range(M_CONV_W):
                o = off + k
                if o % SUBLANES == 0:
                    rows = base[o:o + ls, :]
                else:
                    rows = pltpu.roll(base, MCONV_HIST + ls - o, 0)[0:ls, :]
                acc = acc + rows * cw_ref[k:k + 1, cs]
            xc_scr[j * ls:(j + 1) * ls, cs] = acc * _sigmoid(acc)
    xc = xc_scr[...]
    for j in range(nb):
        tail = cwin_scr[j, lv:lv + MCONV_HIST, :]
        cwin_scr[j, 0:MCONV_HIST, :] = tail
    mcout_ref[...] = cwin_scr[:, 0:MCONV_HIST, :]

    xs = xc[:, :di]
    bm = xc[:, di:di + gn].astype(BF16)
    cm = xc[:, di + gn:].astype(BF16)

    row1 = lax.broadcasted_iota(jnp.int32, (r, 1), 0)
    valid = (row1 & (ls - 1)) < lv
    dt = jax.nn.softplus(dt_ref[...] + dtb_ref[...])
    dt = jnp.where(valid, dt, 0.0)
    a = dt * (-jnp.exp(alog_ref[...]))

    ri = lax.broadcasted_iota(jnp.int32, (r, r), 0)
    ci = lax.broadcasted_iota(jnp.int32, (r, r), 1)
    same = (ri >> ls_shift) == (ci >> ls_shift)
    causal = same & (ci <= ri)
    acs = _sel_right(causal.astype(BF16), a)
    tot = _sel_right(same.astype(BF16), a)
    acs_t = _transpose_id(acs, LANES)
    etot_t = jnp.exp(_transpose_id(tot, LANES))

    expand = (lax.broadcasted_iota(jnp.int32, (LANES, di), 0)
              == (lax.broadcasted_iota(jnp.int32, (LANES, di), 1) >> int(math.log2(hdim)))).astype(BF16)
    dt_rep = _sel_left(dt, expand)
    eacs_rep = _sel_left(jnp.exp(acs), expand)
    wend_rep = _sel_left(jnp.exp(tot - acs) * dt, expand)
    dx = (xs * dt_rep).astype(BF16)
    wdx = xs * wend_rep

    lane = lax.broadcasted_iota(jnp.int32, (r, LANES), 1)
    heads_per_blk = LANES // hdim
    for g in range(ngroups):
        bg = bm[:, g * nstate:(g + 1) * nstate]
        cg = cm[:, g * nstate:(g + 1) * nstate]
        cb = _dot_nt(cg, bg)
        for blk in range(hpg // heads_per_blk):
            col0 = g * gw + blk * LANES
            dxb = dx[:, col0:col0 + LANES]
            yacc = None
            for q in range(heads_per_blk):
                h = g * hpg + blk * heads_per_blk + q
                seg = acs[:, h:h + 1] - acs_t[h:h + 1, :]
                dec = jnp.exp(jnp.where(causal, seg, -jnp.inf))
                m = (cb * dec).astype(BF16)
                hm = (lane >= q * hdim) & (lane < (q + 1) * hdim)
                t = _dot(m, jnp.where(hm, dxb, jnp.zeros_like(dxb)))
                yacc = t if yacc is None else yacc + t
            y_scr[:, col0:col0 + LANES] = yacc

    col_l = lax.broadcasted_iota(jnp.int32, (r, LANES), 0)

    def seq_body(j, carry):
        if nb == 1:
            wdx_j = wdx
            mrow = None
        else:
            mrow = (row1 >> ls_shift) == j
            wdx_j = jnp.where(mrow, wdx, 0.0)
        wdx_j = wdx_j.astype(BF16)
        selc = (col_l == (j + 1) * ls - 1).astype(BF16)
        el = _sel_left(etot_t, selc)
        for g in range(ngroups):
            bg = bm[:, g * nstate:(g + 1) * nstate]
            cg = cm[:, g * nstate:(g + 1) * nstate]
            hg = jnp.concatenate([hout_ref[j, g * hpg + q] for q in range(hpg)], axis=0)
            yoff = _dot_nt(cg, hg.astype(BF16)) * eacs_rep[:, g * gw:(g + 1) * gw]
            if mrow is not None:
                yoff = jnp.where(mrow, yoff, 0.0)
            y_scr[:, g * gw:(g + 1) * gw] += yoff
            st = _dot_tn(wdx_j[:, g * gw:(g + 1) * gw], bg)
            for q in range(hpg):
                h = g * hpg + q
                scale = jnp.broadcast_to(el[h:h + 1, :], (hdim, nstate))
                hout_ref[j, h] = scale * hout_ref[j, h] + st[q * hdim:(q + 1) * hdim, :]
        return carry

    if nb == 1:
        seq_body(0, 0)
    else:
        lax.fori_loop(0, nb, seq_body, 0)

    y = y_scr[...] + dsk_ref[...] * xs
    z = z_ref[...]
    y = y * (z * _sigmoid(z))
    outs = []
    gsz = di // M_GROUPS
    for g in range(M_GROUPS):
        blk = y[:, g * gsz:(g + 1) * gsz]
        outs.append(blk * lax.rsqrt(jnp.mean(blk * blk, axis=-1, keepdims=True) + EPS))
    yn = jnp.concatenate(outs, axis=1) * nw_ref[...]
    yb_ref[...] = _dot(yn.astype(BF16), wout_ref[...])


def _ssd_branch(proj, mcache, h0, p, cols, y_prev, hs_prev, *, h0_layer, out_layer, out_depth,
                row_off, nseq, nb, ls, lv, n_chunks, state_bcast):
    nheads, hdim, nstate = h0.shape[2], h0.shape[3], h0.shape[4]
    di = nheads * hdim
    cd = p["mcw"].shape[1]
    d = p["wmout"].shape[1]
    r = nb * ls
    bsteps = nseq // nb
    blk0 = row_off // r
    const2 = lambda b, i: (0, 0)
    st_map = ((lambda b, i: (h0_layer, 0, 0, 0, 0)) if state_bcast
              else (lambda b, i: (h0_layer, b, 0, 0, 0)))
    mc_map = (lambda b, i: (0, 0, 0)) if state_bcast else (lambda b, i: (b, 0, 0))
    kern = functools.partial(_ssd_kernel, nb=nb, ls=ls, lv=lv, nheads=nheads, hdim=hdim,
                             ngroups=M_GROUPS, nstate=nstate)
    in_specs = [
        pl.BlockSpec((r, cd), lambda b, i: (blk0 + b * n_chunks + i, cols["xbc"] // cd)),
        pl.BlockSpec((r, di), lambda b, i: (blk0 + b * n_chunks + i, cols["z"] // di)),
        pl.BlockSpec((r, LANES), lambda b, i: (blk0 + b * n_chunks + i, cols["dt"] // LANES)),
        pl.BlockSpec((nb, MCONV_HIST, cd), mc_map),
        pl.BlockSpec((None, nb, nheads, hdim, nstate), st_map),
        pl.BlockSpec((MCONV_HIST, cd), const2),
        pl.BlockSpec((1, cd), const2),
        pl.BlockSpec((1, LANES), const2),
        pl.BlockSpec((1, LANES), const2),
        pl.BlockSpec((1, di), const2),
        pl.BlockSpec((1, di), const2),
        pl.BlockSpec((di, d), const2),
    ]
    args = [proj, proj, proj, mcache, h0, p["mcw"], p["mcb"], p["dtb"], p["alog"], p["dskip"],
            p["mnw"], p["wmout"]]
    aliases = {}
    n_in = len(args)
    for prev, out_idx in ((y_prev, 0), (hs_prev, 1)):
        if prev is not None:
            aliases[len(args)] = out_idx
            in_specs.append(pl.BlockSpec(memory_space=pl.ANY))
            args.append(prev)
    for pos in range(n_in, len(args)):
        kern = _skip_ref(kern, pos)
    yb, hout, mcout = pl.pallas_call(
        kern,
        grid=(bsteps, n_chunks),
        in_specs=in_specs,
        out_specs=[
            pl.BlockSpec((r, d), lambda b, i: (blk0 + b * n_chunks + i, 0)),
            pl.BlockSpec((None, nb, nheads, hdim, nstate), lambda b, i: (out_layer, b, 0, 0, 0)),
            pl.BlockSpec((nb, MCONV_HIST, cd), lambda b, i: (b, 0, 0)),
        ],
        out_shape=[
            jax.ShapeDtypeStruct((proj.shape[0], d), F32),
            jax.ShapeDtypeStruct((out_depth, nseq, nheads, hdim, nstate), F32),
            jax.ShapeDtypeStruct((nseq, MCONV_HIST, cd), F32),
        ],
        scratch_shapes=[
            pltpu.VMEM((nb, MCONV_HIST + ls, cd), F32),
            pltpu.VMEM((r, di), F32),
            pltpu.VMEM((r, cd), F32),
        ],
        input_output_aliases=aliases,
        compiler_params=_cparams(("parallel", "arbitrary")),
        name="ssd_branch",
    )(*args)
    return yb, hout, mcout


def _topk_rows(s, k):
    n = s.shape[0]
    rows = lax.broadcasted_iota(jnp.int32, s.shape, 0).astype(F32)
    rank = jnp.full(s.shape, float(k), F32)
    vals = []
    for j in range(k):
        m = jnp.max(s, axis=0, keepdims=True)
        idx = jnp.min(jnp.where(s == m, rows, float(n)), axis=0, keepdims=True)
        hit = rows == idx
        rank = jnp.where(hit, float(j), rank)
        s = jnp.where(hit, -jnp.inf, s)
        vals.append(m)
    return vals, rank


def _cand_pairs(k):
    return [(a, j) for a in range(k) for j in range(k // (a + 1))]


def _gather_rows(src, idxs):
    sub = lax.broadcasted_iota(jnp.int32, (SUBLANES, src.shape[1]), 0)
    groups = {}
    for r, ix in enumerate(idxs):
        if ix is not None:
            groups.setdefault((ix // SUBLANES, (r - ix) % SUBLANES), []).append(r)
    out = None
    for (slab, shift), rs in groups.items():
        piece = src[slab * SUBLANES:(slab + 1) * SUBLANES]
        if shift:
            piece = pltpu.roll(piece, shift, 0)
        if out is None:
            out = piece
        else:
            m = functools.reduce(lambda p, q: p | q, [sub == r for r in rs])
            out = jnp.where(m, piece, out)
    return out


def _row_range_sum(x, lo, hi):
    if hi - lo == 1:
        return x[lo:lo + 1]
    sub = lax.broadcasted_iota(jnp.int32, (SUBLANES, x.shape[1]), 0)
    tot = None
    for sl in range(lo // SUBLANES, (hi - 1) // SUBLANES + 1):
        piece = x[sl * SUBLANES:(sl + 1) * SUBLANES]
        r0 = max(lo, sl * SUBLANES) - sl * SUBLANES
        r1 = min(hi, (sl + 1) * SUBLANES) - sl * SUBLANES
        if (r0, r1) != (0, SUBLANES):
            piece = jnp.where((sub >= r0) & (sub < r1), piece, 0.0)
        tot = piece if tot is None else tot + piece
    return jnp.sum(tot, axis=0, keepdims=True)


def _topk_blind(s, k):
    orig = s
    rank = jnp.full(s.shape, float(k), F32)
    vals = []
    for j in range(k):
        m = jnp.max(s, axis=0, keepdims=True)
        hit = s == m
        rank = jnp.where(hit, float(j), rank)
        s = jnp.where(hit, -jnp.inf, s)
        vals.append(m)
    removed = jnp.sum(jnp.where(s != orig, 1.0, 0.0), axis=0, keepdims=True)
    return vals, rank, jnp.abs(removed - float(k))


def _route_head(s1, s2, k, exact):
    if exact:
        v1, rank1 = _topk_rows(s1, k)
        v2, rank2 = _topk_rows(s2, k)
        ties = jnp.zeros((1, s1.shape[1]), F32)
    else:
        v1, rank1, t1 = _topk_blind(s1, k)
        v2, rank2, t2 = _topk_blind(s2, k)
        ties = t1 + t2
    v1m = jnp.concatenate(v1, axis=0)
    v2m = jnp.concatenate(v2, axis=0)
    pairs = _cand_pairs(k)
    nslab = -(-len(pairs) // SUBLANES)
    padded = pairs + [None] * (nslab * SUBLANES - len(pairs))
    slabs = []
    for sl in range(nslab):
        chunk = padded[sl * SUBLANES:(sl + 1) * SUBLANES]
        slabs.append(_gather_rows(v1m, [None if pr is None else pr[0] for pr in chunk])
                     + _gather_rows(v2m, [None if pr is None else pr[1] for pr in chunk]))
    cand = jnp.concatenate(slabs, axis=0)
    rows = lax.broadcasted_iota(jnp.int32, cand.shape, 0).astype(F32)
    cand = jnp.where(rows < float(len(pairs)), cand, -jnp.inf)
    c = cand
    for _ in range(k):
        m = jnp.max(c, axis=0, keepdims=True)
        if exact:
            idx = jnp.min(jnp.where(c == m, rows, float(cand.shape[0])), axis=0, keepdims=True)
            c = jnp.where(rows == idx, -jnp.inf, c)
        else:
            c = jnp.where(c == m, -jnp.inf, c)
    sel = c != cand
    self32 = sel.astype(F32)
    if not exact:
        ties = ties + jnp.abs(jnp.sum(self32, axis=0, keepdims=True) - float(k))
    cmax = v1[0] + v2[0]
    zsum = jnp.sum(jnp.where(sel, jnp.exp(cand - cmax), 0.0), axis=0, keepdims=True)
    cn = jnp.zeros(s1.shape, F32)
    for a in range(k):
        lo = pairs.index((a, 0))
        hi = lo + k // (a + 1)
        cn = jnp.where(rank1 == float(a), _row_range_sum(self32, lo, hi), cn)
    return cn, 0.5 * jnp.exp(s1 - v1[0]) / zsum, rank2, jnp.exp(s2 - v2[0]), ties


def _merge_route_kernel(x_ref, ya_ref, yb_ref, g_ref, wo_ref, n2_ref, wq_ref, keys_ref,
                        x1_ref, xn_ref, rta_ref, rtb_ref, *, nheads, nkeys, dhalf):
    d = x_ref.shape[1]
    g = _sigmoid(g_ref[...])
    mix = g[:, :d] * ya_ref[...] + g[:, d:] * yb_ref[...]
    x1 = x_ref[...] + _dot(mix.astype(BF16), wo_ref[...])
    x1_ref[...] = x1
    ms = jnp.mean(x1 * x1, axis=-1, keepdims=True)
    xn = (x1 * lax.rsqrt(ms + EPS) * n2_ref[...]).astype(BF16)
    xn_ref[...] = xn
    q = _dot(xn, wq_ref[...]).astype(BF16)

    k = PEER_TOPK
    for h in range(nheads):
        s1 = _dot_nt(keys_ref[h, 0], q[:, (2 * h) * dhalf:(2 * h + 1) * dhalf])
        s2 = _dot_nt(keys_ref[h, 1], q[:, (2 * h + 1) * dhalf:(2 * h + 2) * dhalf])

        def emit(exact, h=h, s1=s1, s2=s2):
            ties = None
            for lg in range(s1.shape[1] // LANES):
                ls_ = slice(lg * LANES, (lg + 1) * LANES)
                cn, e1, rank2, e2, t = _route_head(s1[:, ls_], s2[:, ls_], k, exact)
                rta_ref[h, 0, :, ls_] = cn
                rta_ref[h, 1, :, ls_] = e1
                rtb_ref[h, 0, :, ls_] = rank2.astype(BF16)
                rtb_ref[h, 1, :, ls_] = e2.astype(BF16)
                ties = t if ties is None else jnp.maximum(ties, t)
            return ties

        ties = emit(False)

        @pl.when(jnp.max(ties) > 0.0)
        def _():
            emit(True)


def _merge_route(x, ya, yb, proj, p, cols, tm):
    n, d = x.shape
    nheads, _, nkeys, dhalf = p["keys"].shape
    dq = p["wq"].shape[1]
    const2 = lambda i: (0, 0)
    kern = functools.partial(_merge_route_kernel, nheads=nheads, nkeys=nkeys, dhalf=dhalf)
    return pl.pallas_call(
        kern,
        grid=(n // tm,),
        in_specs=[
            pl.BlockSpec((tm, d), lambda i: (i, 0)),
            pl.BlockSpec((tm, d), lambda i: (i, 0)),
            pl.BlockSpec((tm, d), lambda i: (i, 0)),
            pl.BlockSpec((tm, 2 * d), lambda i: (i, cols["gate"] // (2 * d))),
            pl.BlockSpec((d, d), const2),
            pl.BlockSpec((1, d), const2),
            pl.BlockSpec((d, dq), const2),
            pl.BlockSpec((nheads, 2, nkeys, dhalf), lambda i: (0, 0, 0, 0)),
        ],
        out_specs=[
            pl.BlockSpec((tm, d), lambda i: (i, 0)),
            pl.BlockSpec((tm, d), lambda i: (i, 0)),
            pl.BlockSpec((nheads, 2, nkeys, tm), lambda i: (0, 0, 0, i)),
            pl.BlockSpec((nheads, 2, nkeys, tm), lambda i: (0, 0, 0, i)),
        ],
        out_shape=[
            jax.ShapeDtypeStruct((n, d), F32),
            jax.ShapeDtypeStruct((n, d), BF16),
            jax.ShapeDtypeStruct((nheads, 2, nkeys, n), F32),
            jax.ShapeDtypeStruct((nheads, 2, nkeys, n), BF16),
        ],
        compiler_params=_cparams(("parallel",)),
        name="merge_route",
    )(x, ya, yb, proj, p["wo"], p["n2"], p["wq"], p["keys"])


def _peer_kernel(xn_ref, rta_ref, rtb_ref, u_ref, vt_ref, x1_ref, fw_ref, *rest,
                 nheads, nkeys, nblk, final_norm, split_tile):
    if split_tile is None:
        o_ref, o2_ref = rest[0], None
        acc_ref, ht0, ht1, wt0, wt1 = rest[1:]
    else:
        o_ref, o2_ref = rest[0], rest[1]
        acc_ref, ht0, ht1, wt0, wt1 = rest[2:]
    f = pl.program_id(0)
    eb = u_ref.shape[0]
    na = eb // nkeys
    e_b = lax.rem(jnp.maximum(f - 1, 0), nblk)
    e_c = lax.rem(jnp.maximum(f - 2, 0), nblk)

    @pl.when(f == 0)
    def _():
        for r in (ht0, ht1, wt0, wt1):
            r[...] = jnp.zeros_like(r)

    @pl.when(e_c == 0)
    def _():
        acc_ref[...] = jnp.zeros_like(acc_ref)

    def stages(ht_w, ht_r, wt_w, wt_r):
        h_new = _dot_nt(u_ref[...], xn_ref[...])
        ht_w[...] = h_new
        a0 = pl.multiple_of(e_b * na, na)
        slabs = [(rta_ref[h, 0, pl.ds(a0, na), :], rta_ref[h, 1, pl.ds(a0, na), :])
                 for h in range(nheads)]
        tt = xn_ref.shape[0]
        nv = nkeys // BF16_ROWS
        for al in range(na):
            gates = [None] * nv
            dep = pltpu.bitcast(h_new[(al + 1) * nkeys - SUBLANES:(al + 1) * nkeys, :], jnp.int32)
            zero = lax.shift_right_logical(lax.shift_right_logical(dep, 16), 16).astype(F32)[0:1, :]
            for h in range(nheads):
                cn = jnp.broadcast_to(slabs[h][0][al:al + 1, :] + zero,
                                      (BF16_ROWS, tt)).astype(BF16)
                e1 = jnp.broadcast_to(slabs[h][1][al:al + 1, :], (BF16_ROWS, tt)).astype(BF16)
                for v in range(nv):
                    rk = rtb_ref[h, 0, v * BF16_ROWS:(v + 1) * BF16_ROWS, :]
                    p2 = rtb_ref[h, 1, v * BF16_ROWS:(v + 1) * BF16_ROWS, :]
                    t = jnp.where(rk < cn, p2, jnp.zeros_like(p2)) * e1
                    gates[v] = t if gates[v] is None else gates[v] + t
            for v in range(nv):
                r0 = al * nkeys + v * BF16_ROWS
                hv = ht_r[r0:r0 + BF16_ROWS, :]
                act = hv * (1.0 + lax.erf(hv * (1.0 / math.sqrt(2.0))))
                wt_w[r0:r0 + BF16_ROWS, :] = (gates[v].astype(F32) * act).astype(BF16)
        acc_ref[...] += _dot(vt_ref[...], wt_r[...])

    even = lax.rem(f, 2) == 0

    @pl.when(even)
    def _():
        stages(ht0, ht1, wt1, wt0)

    @pl.when(jnp.logical_not(even))
    def _():
        stages(ht1, ht0, wt0, wt1)

    @pl.when((f >= 2) & (e_c == nblk - 1))
    def _():
        x2 = x1_ref[...] + acc_ref[...].T
        if final_norm:
            ms = jnp.mean(x2 * x2, axis=-1, keepdims=True)
            x2 = x2 * lax.rsqrt(ms + EPS) * fw_ref[...]
        if split_tile is None:
            o_ref[...] = x2
        else:
            tile = (f - 2) // nblk

            @pl.when(tile < split_tile)
            def _():
                o_ref[...] = x2

            @pl.when(tile >= split_tile)
            def _():
                o2_ref[...] = x2


def _peer(xn, rta, rtb, u_bf, vt_bf, x1, fw, tt, eb, final_norm, split_rows=None):
    n, d = x1.shape
    ne = u_bf.shape[0]
    nheads, _, nkeys, _ = rta.shape
    nblk = ne // eb
    nflat = (n // tt) * nblk
    fa = lambda f: jnp.minimum(f, nflat - 1)
    fb = lambda f: jnp.clip(f - 1, 0, nflat - 1)
    fc = lambda f: jnp.clip(f - 2, 0, nflat - 1)
    tile_c = lambda f: fc(f) // nblk
    if split_rows is None:
        split_tile = None
        out_specs = pl.BlockSpec((tt, d), lambda f: (tile_c(f), 0))
        out_shape = jax.ShapeDtypeStruct((n, d), F32)
    else:
        split_tile = split_rows // tt
        out_specs = [
            pl.BlockSpec((tt, d), lambda f: (jnp.minimum(tile_c(f), split_tile - 1), 0)),
            pl.BlockSpec((tt, d), lambda f: (jnp.maximum(tile_c(f) - split_tile, 0), 0)),
        ]
        out_shape = [jax.ShapeDtypeStruct((split_rows, d), F32),
                     jax.ShapeDtypeStruct((n - split_rows, d), F32)]
    kern = functools.partial(_peer_kernel, nheads=nheads, nkeys=nkeys, nblk=nblk,
                             final_norm=final_norm, split_tile=split_tile)
    return pl.pallas_call(
        kern,
        grid=(nflat + 2,),
        in_specs=[
            pl.BlockSpec((tt, d), lambda f: (fa(f) // nblk, 0)),
            pl.BlockSpec((nheads, 2, nkeys, tt), lambda f: (0, 0, 0, fb(f) // nblk)),
            pl.BlockSpec((nheads, 2, nkeys, tt), lambda f: (0, 0, 0, fb(f) // nblk)),
            pl.BlockSpec((eb, d), lambda f: (fa(f) % nblk, 0)),
            pl.BlockSpec((d, eb), lambda f: (0, fc(f) % nblk)),
            pl.BlockSpec((tt, d), lambda f: (fc(f) // nblk, 0)),
            pl.BlockSpec((1, d), lambda f: (0, 0)),
        ],
        out_specs=out_specs,
        out_shape=out_shape,
        scratch_shapes=[
            pltpu.VMEM((d, tt), F32),
            pltpu.VMEM((eb, tt), F32), pltpu.VMEM((eb, tt), F32),
            pltpu.VMEM((eb, tt), BF16), pltpu.VMEM((eb, tt), BF16),
        ],
        compiler_params=_cparams(("arbitrary",)),
        name="peer_experts",
    )(xn, rta, rtb, u_bf, vt_bf, x1, fw)


def _pad_rows(a, n):
    return jnp.pad(a, ((0, n - a.shape[0]),) + ((0, 0),) * (a.ndim - 1))


def kernel(x_prompt, x_sample, cache_conf, cache_mconv, state_ssm, meta_tokens, norm1_w, w_in, b_in, conf_dw_w, conf_dw_b, conf_ln_g, conf_ln_b, w_conf_out, b_conf_out, m_conv_w, m_conv_b, dt_bias, A_log, D_skip, m_norm_w, w_m_out, w_o, norm2_w, peer_w_q, peer_keys, peer_u, peer_v, final_norm_w):
    bp, sp, d = x_prompt.shape
    db, ds, _ = x_sample.shape
    depth = norm1_w.shape[0]
    c_conv = conf_dw_w.shape[2]
    nheads = dt_bias.shape[1]
    di = nheads * M_HEADDIM
    gn = M_GROUPS * M_STATE
    cd = di + 2 * gn
    n_p, n_s = bp * sp, db * ds
    n_all = -(-(n_p + n_s + META_ROWS) // ROW_ALIGN) * ROW_ALIGN
    off_s, off_m = n_p, n_p + n_s

    cols = {"glu_a": 0, "glu_b": c_conv, "z": 2 * c_conv, "xbc": 2 * c_conv + di}
    assert cols["z"] % di == 0 and cols["xbc"] % cd == 0
    cols["gate"] = cols["xbc"] + cd
    assert cols["gate"] % (2 * d) == 0
    cols["dt"] = cols["gate"] + 2 * d
    ncol = cols["dt"] + LANES
    src_dt = 2 * c_conv + di + cd
    src_gate = src_dt + nheads
    tn = ncol // 9 if (ncol % 9 == 0 and (ncol // 9) % LANES == 0) else LANES
    lin_tm = LIN_ROWS if n_all % LIN_ROWS == 0 else ROW_ALIGN

    seq_nb = 8
    prompt_ls = 128
    assert sp % prompt_ls == 0 and db % seq_nb == 0 and n_p % LANES == 0 and n_s % LANES == 0

    x_all = jnp.concatenate([
        x_prompt.reshape(n_p, d), x_sample.reshape(n_s, d), meta_tokens.astype(F32),
        jnp.zeros((n_all - n_p - n_s - N_META, d), F32)], axis=0)

    n_mblk = (n_all - off_m) // META_ROWS
    zc = jnp.zeros((1, CONF_HIST, c_conv), F32)
    zm = jnp.zeros((1, MCONV_HIST, cd), F32)
    zh = jnp.zeros((1, 1, nheads, M_HEADDIM, M_STATE), F32)
    peer_tt = PEER_ROWS if n_all % PEER_ROWS == 0 else ROUTE_ROWS
    zy = lambda: jnp.zeros((n_all, d), F32)
    ssm_p = jnp.zeros((depth, bp, nheads, M_HEADDIM, M_STATE), F32)
    ssm_s = jnp.zeros((depth, db, nheads, M_HEADDIM, M_STATE), F32)
    outs = {k: [] for k in ("conf_p", "mconv_p", "conf_s", "mconv_s")}
    for l in range(depth):
        wl, bl = w_in[l], b_in[l]
        w_re = jnp.concatenate([wl[:, :src_dt], wl[:, src_gate:], wl[:, src_dt:src_gate],
                                jnp.zeros((d, LANES - nheads), F32)], axis=1).astype(BF16)
        b_re = jnp.concatenate([bl[:src_dt], bl[src_gate:], bl[src_dt:src_gate],
                                jnp.zeros((LANES - nheads,), F32)])[None, :]
        proj = _norm_linear(x_all, norm1_w[l][None, :], w_re, b_re, lin_tm, tn)

        pc = {
            "dww": _pad_rows(conf_dw_w[l], CONF_HIST), "dwb": conf_dw_b[l][None, :],
            "lng": conf_ln_g[l][None, :], "lnb": conf_ln_b[l][None, :],
            "wout": w_conf_out[l].astype(BF16), "bout": b_conf_out[l][None, :],
        }
        ps = {
            "mcw": _pad_rows(m_conv_w[l], MCONV_HIST), "mcb": m_conv_b[l][None, :],
            "dtb": jnp.pad(dt_bias[l], (0, LANES - nheads))[None, :],
            "alog": jnp.pad(A_log[l], (0, LANES - nheads))[None, :],
            "dskip": jnp.repeat(D_skip[l], M_HEADDIM)[None, :],
            "mnw": m_norm_w[l][None, :], "wmout": w_m_out[l].astype(BF16),
        }

        ya, cf_m = _conf_branch(proj, zc, pc, zy(), row_off=off_m, nseq=n_mblk, nb=1, ls=META_ROWS,
                                lv=N_META, n_tiles=1, cache_bcast=True)
        yb, h_m, mc_m = _ssd_branch(proj, zm, zh, ps, cols, zy(), None, h0_layer=0, out_layer=0,
                                    out_depth=1, row_off=off_m, nseq=n_mblk, nb=1, ls=META_ROWS,
                                    lv=N_META, n_chunks=1, state_bcast=True)
        ya, cf_p = _conf_branch(proj, cf_m, pc, ya, row_off=0, nseq=bp, nb=1, ls=prompt_ls,
                                lv=prompt_ls, n_tiles=sp // prompt_ls, cache_bcast=True)
        yb, ssm_p, mc_p = _ssd_branch(proj, mc_m, h_m, ps, cols, yb, ssm_p, h0_layer=0, out_layer=l,
                                      out_depth=depth, row_off=0, nseq=bp, nb=1, ls=prompt_ls,
                                      lv=prompt_ls, n_chunks=sp // prompt_ls, state_bcast=True)
        cf_in = jnp.pad(cache_conf[l], ((0, 0), (CONF_HIST - (CONF_W - 1), 0), (0, 0)))
        mc_in = jnp.pad(cache_mconv[l], ((0, 0), (MCONV_HIST - (M_CONV_W - 1), 0), (0, 0)))
        ya, cf_s = _conf_branch(proj, cf_in, pc, ya, row_off=off_s, nseq=db, nb=seq_nb, ls=ds, lv=ds,
                                n_tiles=1, cache_bcast=False)
        yb, ssm_s, mc_s = _ssd_branch(proj, mc_in, state_ssm, ps, cols, yb, ssm_s, h0_layer=l,
                                      out_layer=l, out_depth=depth, row_off=off_s, nseq=db,
                                      nb=seq_nb, ls=ds, lv=ds, n_chunks=1, state_bcast=False)

        pm = {"wo": w_o[l].astype(BF16), "n2": norm2_w[l][None, :], "wq": peer_w_q[l].astype(BF16),
              "keys": peer_keys[l].astype(BF16)}
        x1, xn2, rta, rtb = _merge_route(x_all, ya, yb, proj, pm, cols, ROUTE_ROWS)
        last = l == depth - 1
        x_all = _peer(xn2, rta, rtb, peer_u[l].astype(BF16), peer_v[l].T.astype(BF16), x1,
                      final_norm_w[None, :], peer_tt, PEER_KEY_ROWS * peer_keys.shape[3], last,
                      split_rows=n_p if (last and n_p % peer_tt == 0) else None)

        outs["conf_p"].append(cf_p[:, CONF_HIST - (CONF_W - 1):])
        outs["mconv_p"].append(mc_p[:, MCONV_HIST - (M_CONV_W - 1):])
        outs["conf_s"].append(cf_s[:, CONF_HIST - (CONF_W - 1):])
        outs["mconv_s"].append(mc_s[:, MCONV_HIST - (M_CONV_W - 1):])

    if isinstance(x_all, (list, tuple)):
        y_prompt = x_all[0].reshape(bp, sp, d)
        y_sample = x_all[1][:n_s].reshape(db, ds, d)
    else:
        y_prompt = x_all[:n_p].reshape(bp, sp, d)
        y_sample = x_all[off_s:off_s + n_s].reshape(db, ds, d)
    return (y_prompt, y_sample,
            jnp.stack(outs["conf_p"]), jnp.stack(outs["mconv_p"]), ssm_p,
            jnp.stack(outs["conf_s"]), jnp.stack(outs["mconv_s"]), ssm_s)
```

```python
import functools
import math

import jax
import jax.numpy as jnp
from jax import lax
from jax.experimental import pallas as pl
from jax.experimental.pallas import tpu as pltpu

F32 = jnp.float32
BF16 = jnp.bfloat16

EPS = 1e-6
N_META = 16
CONF_W = 31
M_CONV_W = 4
M_HEADDIM = 64
M_GROUPS = 8
M_STATE = 128
PEER_TOPK = 16
LANES = 128
SUBLANES = 8
BF16_ROWS = 16
CONF_HIST = 32
MCONV_HIST = 8
META_ROWS = 128
ROW_ALIGN = 512
LIN_ROWS = 1280
ROUTE_ROWS = 256
PEER_ROWS = 512
PEER_KEY_ROWS = 8
VMEM_LIMIT = 56 * 1024 * 1024


def _cparams(sem):
    return pltpu.CompilerParams(dimension_semantics=sem, vmem_limit_bytes=VMEM_LIMIT)


def _split3(a):
    p1 = a.astype(BF16)
    r1 = a - p1.astype(F32)
    p2 = r1.astype(BF16)
    r2 = r1 - p2.astype(F32)
    p3 = r2.astype(BF16)
    return p1, p2, p3


def _dot(a, b):
    return jnp.dot(a, b, preferred_element_type=F32)


def _dot_nt(a, b):
    return lax.dot_general(a, b, (((1,), (1,)), ((), ())), preferred_element_type=F32)


def _dot_tn(a, b):
    return lax.dot_general(a, b, (((0,), (0,)), ((), ())), preferred_element_type=F32)


def _sel_right(m01, a):
    p1, p2, p3 = _split3(a)
    return _dot(m01, p1) + _dot(m01, p2) + _dot(m01, p3)


def _sel_left(a, m01):
    p1, p2, p3 = _split3(a)
    return _dot(p1, m01) + _dot(p2, m01) + _dot(p3, m01)


def _transpose_id(a, n):
    eye = (lax.broadcasted_iota(jnp.int32, (n, n), 0)
           == lax.broadcasted_iota(jnp.int32, (n, n), 1)).astype(BF16)
    p1, p2, p3 = _split3(a)
    return _dot_nt(eye, p1) + _dot_nt(eye, p2) + _dot_nt(eye, p3)


def _sigmoid(x):
    return 1.0 / (1.0 + jnp.exp(-x))


def _lin_kernel(x_ref, nw_ref, w_ref, b_ref, o_ref, xn_ref):
    @pl.when(pl.program_id(1) == 0)
    def _():
        x = x_ref[...]
        ms = jnp.mean(x * x, axis=-1, keepdims=True)
        xn_ref[...] = (x * lax.rsqrt(ms + EPS) * nw_ref[...]).astype(BF16)

    o_ref[...] = _dot(xn_ref[...], w_ref[...]) + b_ref[...]


def _norm_linear(x, nw, w, b, tm, tn):
    n, d = x.shape
    nc = w.shape[1]
    return pl.pallas_call(
        _lin_kernel,
        grid=(n // tm, nc // tn),
        in_specs=[
            pl.BlockSpec((tm, d), lambda i, j: (i, 0)),
            pl.BlockSpec((1, d), lambda i, j: (0, 0)),
            pl.BlockSpec((d, tn), lambda i, j: (0, j)),
            pl.BlockSpec((1, tn), lambda i, j: (0, j)),
        ],
        out_specs=pl.BlockSpec((tm, tn), lambda i, j: (i, j)),
        out_shape=jax.ShapeDtypeStruct((n, nc), F32),
        scratch_shapes=[pltpu.VMEM((tm, d), BF16)],
        compiler_params=_cparams(("parallel", "arbitrary")),
        name="norm_linear",
    )(x, nw, w, b)


def _conf_kernel(ga_ref, gb_ref, cache_ref, dww_ref, dwb_ref, lng_ref, lnb_ref, wout_ref, bout_ref,
                 ya_ref, hist_ref, win_ref, c_scr, *, nb, ls, lv):
    @pl.when(pl.program_id(1) == 0)
    def _():
        win_ref[:, 0:CONF_HIST, :] = cache_ref[...]

    u = ga_ref[...] * _sigmoid(gb_ref[...])
    for j in range(nb):
        win_ref[j, CONF_HIST:CONF_HIST + ls, :] = u[j * ls:(j + 1) * ls, :]

    off = CONF_HIST - (CONF_W - 1)
    for j in range(nb):
        for cb in range(dwb_ref.shape[1] // LANES):
            cs = slice(cb * LANES, (cb + 1) * LANES)
            acc = jnp.broadcast_to(dwb_ref[:, cs], (ls, LANES))
            for ph in range(SUBLANES):
                taps = [k for k in range(CONF_W) if (k + off) % SUBLANES == ph]
                if not taps:
                    continue
                span = max((k + off) // SUBLANES for k in taps) * SUBLANES + ls
                if ph:
                    view = pltpu.roll(win_ref[j, 0:span + SUBLANES, cs], span + SUBLANES - ph, 0)
                for k in taps:
                    m = (k + off) // SUBLANES * SUBLANES
                    rows = view[m:m + ls, :] if ph else win_ref[j, m:m + ls, cs]
                    acc = acc + rows * dww_ref[k:k + 1, cs]
            c_scr[j * ls:(j + 1) * ls, cs] = acc
    c = c_scr[...]

    mu = jnp.mean(c, axis=-1, keepdims=True)
    xc = c - mu
    y = xc * lax.rsqrt(jnp.mean(xc * xc, axis=-1, keepdims=True) + EPS)
    y = y * lng_ref[...] + lnb_ref[...]
    y = y * _sigmoid(y)
    ya_ref[...] = _dot(y.astype(BF16), wout_ref[...]) + bout_ref[...]

    for j in range(nb):
        tail = win_ref[j, lv:lv + CONF_HIST, :]
        win_ref[j, 0:CONF_HIST, :] = tail
    hist_ref[...] = win_ref[:, 0:CONF_HIST, :]


def _skip_ref(kern, pos):
    def wrapped(*refs):
        return kern(*refs[:pos], *refs[pos + 1:])
    return wrapped


def _conf_branch(proj, cache, p, y_prev, *, row_off, nseq, nb, ls, lv, n_tiles, cache_bcast):
    c = p["dww"].shape[1]
    d = p["wout"].shape[1]
    r = nb * ls
    bsteps = nseq // nb
    blk0 = row_off // r
    row_map = lambda b, i: (blk0 + b * n_tiles + i, 0)
    cache_map = (lambda b, i: (0, 0, 0)) if cache_bcast else (lambda b, i: (b, 0, 0))
    const2 = lambda b, i: (0, 0)
    kern = functools.partial(_conf_kernel, nb=nb, ls=ls, lv=lv)
    in_specs = [
        pl.BlockSpec((r, c), row_map),
        pl.BlockSpec((r, c), lambda b, i: (blk0 + b * n_tiles + i, 1)),
        pl.BlockSpec((nb, CONF_HIST, c), cache_map),
        pl.BlockSpec((CONF_HIST, c), const2),
        pl.BlockSpec((1, c), const2),
        pl.BlockSpec((1, c), const2),
        pl.BlockSpec((1, c), const2),
        pl.BlockSpec((c, d), const2),
        pl.BlockSpec((1, d), const2),
    ]
    args = [proj, proj, cache, p["dww"], p["dwb"], p["lng"], p["lnb"], p["wout"], p["bout"]]
    aliases = {}
    if y_prev is not None:
        aliases = {len(args): 0}
        kern = _skip_ref(kern, len(args))
        in_specs.append(pl.BlockSpec(memory_space=pl.ANY))
        args.append(y_prev)
    ya, hist = pl.pallas_call(
        kern,
        grid=(bsteps, n_tiles),
        in_specs=in_specs,
        out_specs=[
            pl.BlockSpec((r, d), row_map),
            pl.BlockSpec((nb, CONF_HIST, c), lambda b, i: (b, 0, 0)),
        ],
        out_shape=[
            jax.ShapeDtypeStruct((proj.shape[0], d), F32),
            jax.ShapeDtypeStruct((nseq, CONF_HIST, c), F32),
        ],
        scratch_shapes=[pltpu.VMEM((nb, CONF_HIST + ls, c), F32), pltpu.VMEM((r, c), F32)],
        input_output_aliases=aliases,
        compiler_params=_cparams(("parallel", "arbitrary")),
        name="conformer_branch",
    )(*args)
    return ya, hist


def _ssd_kernel(xbc_ref, z_ref, dt_ref, mc_ref, h0_ref, cw_ref, cb_ref, dtb_ref, alog_ref, dsk_ref,
                nw_ref, wout_ref, yb_ref, hout_ref, mcout_ref, cwin_scr, y_scr, xc_scr,
                *, nb, ls, lv, nheads, hdim, ngroups, nstate):
    r = nb * ls
    di = nheads * hdim
    gn = ngroups * nstate
    hpg = nheads // ngroups
    gw = hpg * hdim
    ls_shift = int(math.log2(ls))
    i = pl.program_id(1)

    @pl.when(i == 0)
    def _():
        hout_ref[...] = h0_ref[...]
        cwin_scr[:, 0:MCONV_HIST, :] = mc_ref[...]

    xbc = xbc_ref[...]
    for j in range(nb):
        cwin_scr[j, MCONV_HIST:MCONV_HIST + ls, :] = xbc[j * ls:(j + 1) * ls, :]
    off = MCONV_HIST - (M_CONV_W - 1)
    for j in range(nb):
        for cblk in range(cb_ref.shape[1] // LANES):
            cs = slice(cblk * LANES, (cblk + 1) * LANES)
            base = cwin_scr[j, :, cs]
            acc = jnp.broadcast_to(cb_ref[:, cs], (ls, LANES))
            for k in range(M_CONV_W):
                o = off + k
                if o % SUBLANES == 0:
                    rows = base[o:o + ls, :]
                else:
                    rows = pltpu.roll(base, MCONV_HIST + ls - o, 0)[0:ls, :]
                acc = acc + rows * cw_ref[k:k + 1, cs]
            xc_scr[j * ls:(j + 1) * ls, cs] = acc * _sigmoid(acc)
    xc = xc_scr[...]
    for j in range(nb):
        tail = cwin_scr[j, lv:lv + MCONV_HIST, :]
        cwin_scr[j, 0:MCONV_HIST, :] = tail
    mcout_ref[...] = cwin_scr[:, 0:MCONV_HIST, :]

    xs = xc[:, :di]
    bm = xc[:, di:di + gn].astype(BF16)
    cm = xc[:, di + gn:].astype(BF16)

    row1 = lax.broadcasted_iota(jnp.int32, (r, 1), 0)
    valid = (row1 & (ls - 1)) < lv
    dt = jax.nn.softplus(dt_ref[...] + dtb_ref[...])
    dt = jnp.where(valid, dt, 0.0)
    a = dt * (-jnp.exp(alog_ref[...]))

    ri = lax.broadcasted_iota(jnp.int32, (r, r), 0)
    ci = lax.broadcasted_iota(jnp.int32, (r, r), 1)
    same = (ri >> ls_shift) == (ci >> ls_shift)
    causal = same & (ci <= ri)
    acs = _sel_right(causal.astype(BF16), a)
    tot = _sel_right(same.astype(BF16), a)
    acs_t = _transpose_id(acs, LANES)
    etot_t = jnp.exp(_transpose_id(tot, LANES))

    expand = (lax.broadcasted_iota(jnp.int32, (LANES, di), 0)
              == (lax.broadcasted_iota(jnp.int32, (LANES, di), 1) >> int(math.log2(hdim)))).astype(BF16)
    dt_rep = _sel_left(dt, expand)
    eacs_rep = _sel_left(jnp.exp(acs), expand)
    wend_rep = _sel_left(jnp.exp(tot - acs) * dt, expand)
    dx = (xs * dt_rep).astype(BF16)
    wdx = xs * wend_rep

    lane = lax.broadcasted_iota(jnp.int32, (r, LANES), 1)
    heads_per_blk = LANES // hdim
    for g in range(ngroups):
        bg = bm[:, g * nstate:(g + 1) * nstate]
        cg = cm[:, g * nstate:(g + 1) * nstate]
        cb = _dot_nt(cg, bg)
        for blk in range(hpg // heads_per_blk):
            col0 = g * gw + blk * LANES
            dxb = dx[:, col0:col0 + LANES]
            yacc = None
            for q in range(heads_per_blk):
                h = g * hpg + blk * heads_per_blk + q
                seg = acs[:, h:h + 1] - acs_t[h:h + 1, :]
                dec = jnp.exp(jnp.where(causal, seg, -jnp.inf))
                m = (cb * dec).astype(BF16)
                hm = (lane >= q * hdim) & (lane < (q + 1) * hdim)
                t = _dot(m, jnp.where(hm, dxb, jnp.zeros_like(dxb)))
                yacc = t if yacc is None else yacc + t
            y_scr[:, col0:col0 + LANES] = yacc

    col_l = lax.broadcasted_iota(jnp.int32, (r, LANES), 0)

    def seq_body(j, carry):
        if nb == 1:
            wdx_j = wdx
            mrow = None
        else:
            mrow = (row1 >> ls_shift) == j
            wdx_j = jnp.where(mrow, wdx, 0.0)
        wdx_j = wdx_j.astype(BF16)
        selc = (col_l == (j + 1) * ls - 1).astype(BF16)
        el = _sel_left(etot_t, selc)
        for g in range(ngroups):
            bg = bm[:, g * nstate:(g + 1) * nstate]
            cg = cm[:, g * nstate:(g + 1) * nstate]
            hg = jnp.concatenate([hout_ref[j, g * hpg + q] for q in range(hpg)], axis=0)
            yoff = _dot_nt(cg, hg.astype(BF16)) * eacs_rep[:, g * gw:(g + 1) * gw]
            if mrow is not None:
                yoff = jnp.where(mrow, yoff, 0.0)
            y_scr[:, g * gw:(g + 1) * gw] += yoff
            st = _dot_tn(wdx_j[:, g * gw:(g + 1) * gw], bg)
            for q in range(hpg):
                h = g * hpg + q
                scale = jnp.broadcast_to(el[h:h + 1, :], (hdim, nstate))
                hout_ref[j, h] = scale * hout_ref[j, h] + st[q * hdim:(q + 1) * hdim, :]
        return carry

    if nb == 1:
        seq_body(0, 0)
    else:
        lax.fori_loop(0, nb, seq_body, 0)

    y = y_scr[...] + dsk_ref[...] * xs
    z = z_ref[...]
    y = y * (z * _sigmoid(z))
    outs = []
    gsz = di // M_GROUPS
    for g in range(M_GROUPS):
        blk = y[:, g * gsz:(g + 1) * gsz]
        outs.append(blk * lax.rsqrt(jnp.mean(blk * blk, axis=-1, keepdims=True) + EPS))
    yn = jnp.concatenate(outs, axis=1) * nw_ref[...]
    yb_ref[...] = _dot(yn.astype(BF16), wout_ref[...])


def _ssd_branch(proj, mcache, h0, p, cols, y_prev, hs_prev, *, h0_layer, out_layer, out_depth,
                row_off, nseq, nb, ls, lv, n_chunks, state_bcast):
    nheads, hdim, nstate = h0.shape[2], h0.shape[3], h0.shape[4]
    di = nheads * hdim
    cd = p["mcw"].shape[1]
    d = p["wmout"].shape[1]
    r = nb * ls
    bsteps = nseq // nb
    blk0 = row_off // r
    const2 = lambda b, i: (0, 0)
    st_map = ((lambda b, i: (h0_layer, 0, 0, 0, 0)) if state_bcast
              else (lambda b, i: (h0_layer, b, 0, 0, 0)))
    mc_map = (lambda b, i: (0, 0, 0)) if state_bcast else (lambda b, i: (b, 0, 0))
    kern = functools.partial(_ssd_kernel, nb=nb, ls=ls, lv=lv, nheads=nheads, hdim=hdim,
                             ngroups=M_GROUPS, nstate=nstate)
    in_specs = [
        pl.BlockSpec((r, cd), lambda b, i: (blk0 + b * n_chunks + i, cols["xbc"] // cd)),
        pl.BlockSpec((r, di), lambda b, i: (blk0 + b * n_chunks + i, cols["z"] // di)),
        pl.BlockSpec((r, LANES), lambda b, i: (blk0 + b * n_chunks + i, cols["dt"] // LANES)),
        pl.BlockSpec((nb, MCONV_HIST, cd), mc_map),
        pl.BlockSpec((None, nb, nheads, hdim, nstate), st_map),
        pl.BlockSpec((MCONV_HIST, cd), const2),
        pl.BlockSpec((1, cd), const2),
        pl.BlockSpec((1, LANES), const2),
        pl.BlockSpec((1, LANES), const2),
        pl.BlockSpec((1, di), const2),
        pl.BlockSpec((1, di), const2),
        pl.BlockSpec((di, d), const2),
    ]
    args = [proj, proj, proj, mcache, h0, p["mcw"], p["mcb"], p["dtb"], p["alog"], p["dskip"],
            p["mnw"], p["wmout"]]
    aliases = {}
    n_in = len(args)
    for prev, out_idx in ((y_prev, 0), (hs_prev, 1)):
        if prev is not None:
            aliases[len(args)] = out_idx
            in_specs.append(pl.BlockSpec(memory_space=pl.ANY))
            args.append(prev)
    for pos in range(n_in, len(args)):
        kern = _skip_ref(kern, pos)
    yb, hout, mcout = pl.pallas_call(
        kern,
        grid=(bsteps, n_chunks),
        in_specs=in_specs,
        out_specs=[
            pl.BlockSpec((r, d), lambda b, i: (blk0 + b * n_chunks + i, 0)),
            pl.BlockSpec((None, nb, nheads, hdim, nstate), lambda b, i: (out_layer, b, 0, 0, 0)),
            pl.BlockSpec((nb, MCONV_HIST, cd), lambda b, i: (b, 0, 0)),
        ],
        out_shape=[
            jax.ShapeDtypeStruct((proj.shape[0], d), F32),
            jax.ShapeDtypeStruct((out_depth, nseq, nheads, hdim, nstate), F32),
            jax.ShapeDtypeStruct((nseq, MCONV_HIST, cd), F32),
        ],
        scratch_shapes=[
            pltpu.VMEM((nb, MCONV_HIST + ls, cd), F32),
            pltpu.VMEM((r, di), F32),
            pltpu.VMEM((r, cd), F32),
        ],
        input_output_aliases=aliases,
        compiler_params=_cparams(("parallel", "arbitrary")),
        name="ssd_branch",
    )(*args)
    return yb, hout, mcout


def _topk_rows(s, k):
    n = s.shape[0]
    rows = lax.broadcasted_iota(jnp.int32, s.shape, 0).astype(F32)
    rank = jnp.full(s.shape, float(k), F32)
    vals = []
    for j in range(k):
        m = jnp.max(s, axis=0, keepdims=True)
        idx = jnp.min(jnp.where(s == m, rows, float(n)), axis=0, keepdims=True)
        hit = rows == idx
        rank = jnp.where(hit, float(j), rank)
        s = jnp.where(hit, -jnp.inf, s)
        vals.append(m)
    return vals, rank


def _cand_pairs(k):
    return [(a, j) for a in range(k) for j in range(k // (a + 1))]


def _gather_rows(src, idxs):
    sub = lax.broadcasted_iota(jnp.int32, (SUBLANES, src.shape[1]), 0)
    groups = {}
    for r, ix in enumerate(idxs):
        if ix is not None:
            groups.setdefault((ix // SUBLANES, (r - ix) % SUBLANES), []).append(r)
    out = None
    for (slab, shift), rs in groups.items():
        piece = src[slab * SUBLANES:(slab + 1) * SUBLANES]
        if shift:
            piece = pltpu.roll(piece, shift, 0)
        if out is None:
            out = piece
        else:
            m = functools.reduce(lambda p, q: p | q, [sub == r for r in rs])
            out = jnp.where(m, piece, out)
    return out


def _row_range_sum(x, lo, hi):
    if hi - lo == 1:
        return x[lo:lo + 1]
    sub = lax.broadcasted_iota(jnp.int32, (SUBLANES, x.shape[1]), 0)
    tot = None
    for sl in range(lo // SUBLANES, (hi - 1) // SUBLANES + 1):
        piece = x[sl * SUBLANES:(sl + 1) * SUBLANES]
        r0 = max(lo, sl * SUBLANES) - sl * SUBLANES
        r1 = min(hi, (sl + 1) * SUBLANES) - sl * SUBLANES
        if (r0, r1) != (0, SUBLANES):
            piece = jnp.where((sub >= r0) & (sub < r1), piece, 0.0)
        tot = piece if tot is None else tot + piece
    return jnp.sum(tot, axis=0, keepdims=True)


def _topk_blind(s, k):
    orig = s
    rank = jnp.full(s.shape, float(k), F32)
    vals = []
    for j in range(k):
        m = jnp.max(s, axis=0, keepdims=True)
        hit = s == m
        rank = jnp.where(hit, float(j), rank)
        s = jnp.where(hit, -jnp.inf, s)
        vals.append(m)
    removed = jnp.sum(jnp.where(s != orig, 1.0, 0.0), axis=0, keepdims=True)
    return vals, rank, jnp.abs(removed - float(k))


def _route_head(s1, s2, k, exact):
    if exact:
        v1, rank1 = _topk_rows(s1, k)
        v2, rank2 = _topk_rows(s2, k)
        ties = jnp.zeros((1, s1.shape[1]), F32)
    else:
        v1, rank1, t1 = _topk_blind(s1, k)
        v2, rank2, t2 = _topk_blind(s2, k)
        ties = t1 + t2
    v1m = jnp.concatenate(v1, axis=0)
    v2m = jnp.concatenate(v2, axis=0)
    pairs = _cand_pairs(k)
    nslab = -(-len(pairs) // SUBLANES)
    padded = pairs + [None] * (nslab * SUBLANES - len(pairs))
    slabs = []
    for sl in range(nslab):
        chunk = padded[sl * SUBLANES:(sl + 1) * SUBLANES]
        slabs.append(_gather_rows(v1m, [None if pr is None else pr[0] for pr in chunk])
                     + _gather_rows(v2m, [None if pr is None else pr[1] for pr in chunk]))
    cand = jnp.concatenate(slabs, axis=0)
    rows = lax.broadcasted_iota(jnp.int32, cand.shape, 0).astype(F32)
    cand = jnp.where(rows < float(len(pairs)), cand, -jnp.inf)
    c = cand
    for _ in range(k):
        m = jnp.max(c, axis=0, keepdims=True)
        if exact:
            idx = jnp.min(jnp.where(c == m, rows, float(cand.shape[0])), axis=0, keepdims=True)
            c = jnp.where(rows == idx, -jnp.inf, c)
        else:
            c = jnp.where(c == m, -jnp.inf, c)
    sel = c != cand
    self32 = sel.astype(F32)
    if not exact:
        ties = ties + jnp.abs(jnp.sum(self32, axis=0, keepdims=True) - float(k))
    cmax = v1[0] + v2[0]
    zsum = jnp.sum(jnp.where(sel, jnp.exp(cand - cmax), 0.0), axis=0, keepdims=True)
    cn = jnp.zeros(s1.shape, F32)
    for a in range(k):
        lo = pairs.index((a, 0))
        hi = lo + k // (a + 1)
        cn = jnp.where(rank1 == float(a), _row_range_sum(self32, lo, hi), cn)
    return cn, 0.5 * jnp.exp(s1 - v1[0]) / zsum, rank2, jnp.exp(s2 - v2[0]), ties


def _merge_route_kernel(x_ref, ya_ref, yb_ref, g_ref, wo_ref, n2_ref, wq_ref, keys_ref,
                        x1_ref, xn_ref, rta_ref, rtb_ref, *, nheads, nkeys, dhalf):
    d = x_ref.shape[1]
    g = _sigmoid(g_ref[...])
    mix = g[:, :d] * ya_ref[...] + g[:, d:] * yb_ref[...]
    x1 = x_ref[...] + _dot(mix.astype(BF16), wo_ref[...])
    x1_ref[...] = x1
    ms = jnp.mean(x1 * x1, axis=-1, keepdims=True)
    xn = (x1 * lax.rsqrt(ms + EPS) * n2_ref[...]).astype(BF16)
    xn_ref[...] = xn
    q = _dot(xn, wq_ref[...]).astype(BF16)

    k = PEER_TOPK
    for h in range(nheads):
        s1 = _dot_nt(keys_ref[h, 0], q[:, (2 * h) * dhalf:(2 * h + 1) * dhalf])
        s2 = _dot_nt(keys_ref[h, 1], q[:, (2 * h + 1) * dhalf:(2 * h + 2) * dhalf])

        def emit(exact, h=h, s1=s1, s2=s2):
            ties = None
            for lg in range(s1.shape[1] // LANES):
                ls_ = slice(lg * LANES, (lg + 1) * LANES)
                cn, e1, rank2, e2, t = _route_head(s1[:, ls_], s2[:, ls_], k, exact)
                rta_ref[h, 0, :, ls_] = cn
                rta_ref[h, 1, :, ls_] = e1
                rtb_ref[h, 0, :, ls_] = rank2.astype(BF16)
                rtb_ref[h, 1, :, ls_] = e2.astype(BF16)
                ties = t if ties is None else jnp.maximum(ties, t)
            return ties

        ties = emit(False)

        @pl.when(jnp.max(ties) > 0.0)
        def _():
            emit(True)


def _merge_route(x, ya, yb, proj, p, cols, tm):
    n, d = x.shape
    nheads, _, nkeys, dhalf = p["keys"].shape
    dq = p["wq"].shape[1]
    const2 = lambda i: (0, 0)
    kern = functools.partial(_merge_route_kernel, nheads=nheads, nkeys=nkeys, dhalf=dhalf)
    return pl.pallas_call(
        kern,
        grid=(n // tm,),
        in_specs=[
            pl.BlockSpec((tm, d), lambda i: (i, 0)),
            pl.BlockSpec((tm, d), lambda i: (i, 0)),
            pl.BlockSpec((tm, d), lambda i: (i, 0)),
            pl.BlockSpec((tm, 2 * d), lambda i: (i, cols["gate"] // (2 * d))),
            pl.BlockSpec((d, d), const2),
            pl.BlockSpec((1, d), const2),
            pl.BlockSpec((d, dq), const2),
            pl.BlockSpec((nheads, 2, nkeys, dhalf), lambda i: (0, 0, 0, 0)),
        ],
        out_specs=[
            pl.BlockSpec((tm, d), lambda i: (i, 0)),
            pl.BlockSpec((tm, d), lambda i: (i, 0)),
            pl.BlockSpec((nheads, 2, nkeys, tm), lambda i: (0, 0, 0, i)),
            pl.BlockSpec((nheads, 2, nkeys, tm), lambda i: (0, 0, 0, i)),
        ],
        out_shape=[
            jax.ShapeDtypeStruct((n, d), F32),
            jax.ShapeDtypeStruct((n, d), BF16),
            jax.ShapeDtypeStruct((nheads, 2, nkeys, n), F32),
            jax.ShapeDtypeStruct((nheads, 2, nkeys, n), BF16),
        ],
        compiler_params=_cparams(("parallel",)),
        name="merge_route",
    )(x, ya, yb, proj, p["wo"], p["n2"], p["wq"], p["keys"])


def _peer_kernel(xn_ref, rta_ref, rtb_ref, u_ref, vt_ref, x1_ref, fw_ref, *rest,
                 nheads, nkeys, nblk, final_norm, split_tile):
    if split_tile is None:
        o_ref, o2_ref = rest[0], None
        acc_ref, ht0, ht1, wt0, wt1 = rest[1:]
    else:
        o_ref, o2_ref = rest[0], rest[1]
        acc_ref, ht0, ht1, wt0, wt1 = rest[2:]
    f = pl.program_id(0)
    eb = u_ref.shape[0]
    na = eb // nkeys
    e_b = lax.rem(jnp.maximum(f - 1, 0), nblk)
    e_c = lax.rem(jnp.maximum(f - 2, 0), nblk)

    @pl.when(f == 0)
    def _():
        for r in (ht0, ht1, wt0, wt1):
            r[...] = jnp.zeros_like(r)

    @pl.when(e_c == 0)
    def _():
        acc_ref[...] = jnp.zeros_like(acc_ref)

    def stages(ht_w, ht_r, wt_w, wt_r):
        ht_w[...] = _dot_nt(u_ref[...], xn_ref[...])
        a0 = pl.multiple_of(e_b * na, na)
        slabs = [(rta_ref[h, 0, pl.ds(a0, na), :], rta_ref[h, 1, pl.ds(a0, na), :])
                 for h in range(nheads)]
        tt = xn_ref.shape[0]
        nv = nkeys // BF16_ROWS
        for al in range(na):
            gates = [None] * nv
            for h in range(nheads):
                cn = jnp.broadcast_to(slabs[h][0][al:al + 1, :], (BF16_ROWS, tt)).astype(BF16)
                e1 = jnp.broadcast_to(slabs[h][1][al:al + 1, :], (BF16_ROWS, tt)).astype(BF16)
                for v in range(nv):
                    rk = rtb_ref[h, 0, v * BF16_ROWS:(v + 1) * BF16_ROWS, :]
                    p2 = rtb_ref[h, 1, v * BF16_ROWS:(v + 1) * BF16_ROWS, :]
                    t = jnp.where(rk < cn, p2, jnp.zeros_like(p2)) * e1
                    gates[v] = t if gates[v] is None else gates[v] + t
            for v in range(nv):
                r0 = al * nkeys + v * BF16_ROWS
                hv = ht_r[r0:r0 + BF16_ROWS, :]
                act = hv * (1.0 + lax.erf(hv * (1.0 / math.sqrt(2.0))))
                wt_w[r0:r0 + BF16_ROWS, :] = (gates[v].astype(F32) * act).astype(BF16)
        acc_ref[...] += _dot(vt_ref[...], wt_r[...])

    even = lax.rem(f, 2) == 0

    @pl.when(even)
    def _():
        stages(ht0, ht1, wt1, wt0)

    @pl.when(jnp.logical_not(even))
    def _():
        stages(ht1, ht0, wt0, wt1)

    @pl.when((f >= 2) & (e_c == nblk - 1))
    def _():
        x2 = x1_ref[...] + acc_ref[...].T
        if final_norm:
            ms = jnp.mean(x2 * x2, axis=-1, keepdims=True)
            x2 = x2 * lax.rsqrt(ms + EPS) * fw_ref[...]
        if split_tile is None:
            o_ref[...] = x2
        else:
            tile = (f - 2) // nblk

            @pl.when(tile < split_tile)
            def _():
                o_ref[...] = x2

            @pl.when(tile >= split_tile)
            def _():
                o2_ref[...] = x2


def _peer(xn, rta, rtb, u_bf, vt_bf, x1, fw, tt, eb, final_norm, split_rows=None):
    n, d = x1.shape
    ne = u_bf.shape[0]
    nheads, _, nkeys, _ = rta.shape
    nblk = ne // eb
    nflat = (n // tt) * nblk
    fa = lambda f: jnp.minimum(f, nflat - 1)
    fb = lambda f: jnp.clip(f - 1, 0, nflat - 1)
    fc = lambda f: jnp.clip(f - 2, 0, nflat - 1)
    tile_c = lambda f: fc(f) // nblk
    if split_rows is None:
        split_tile = None
        out_specs = pl.BlockSpec((tt, d), lambda f: (tile_c(f), 0))
        out_shape = jax.ShapeDtypeStruct((n, d), F32)
    else:
        split_tile = split_rows // tt
        out_specs = [
            pl.BlockSpec((tt, d), lambda f: (jnp.minimum(tile_c(f), split_tile - 1), 0)),
            pl.BlockSpec((tt, d), lambda f: (jnp.maximum(tile_c(f) - split_tile, 0), 0)),
        ]
        out_shape = [jax.ShapeDtypeStruct((split_rows, d), F32),
                     jax.ShapeDtypeStruct((n - split_rows, d), F32)]
    kern = functools.partial(_peer_kernel, nheads=nheads, nkeys=nkeys, nblk=nblk,
                             final_norm=final_norm, split_tile=split_tile)
    return pl.pallas_call(
        kern,
        grid=(nflat + 2,),
        in_specs=[
            pl.BlockSpec((tt, d), lambda f: (fa(f) // nblk, 0)),
            pl.BlockSpec((nheads, 2, nkeys, tt), lambda f: (0, 0, 0, fb(f) // nblk)),
            pl.BlockSpec((nheads, 2, nkeys, tt), lambda f: (0, 0, 0, fb(f) // nblk)),
            pl.BlockSpec((eb, d), lambda f: (fa(f) % nblk, 0)),
            pl.BlockSpec((d, eb), lambda f: (0, fc(f) % nblk)),
            pl.BlockSpec((tt, d), lambda f: (fc(f) // nblk, 0)),
            pl.BlockSpec((1, d), lambda f: (0, 0)),
        ],
        out_specs=out_specs,
        out_shape=out_shape,
        scratch_shapes=[
            pltpu.VMEM((d, tt), F32),
            pltpu.VMEM((eb, tt), F32), pltpu.VMEM((eb, tt), F32),
            pltpu.VMEM((eb, tt), BF16), pltpu.VMEM((eb, tt), BF16),
        ],
        compiler_params=_cparams(("arbitrary",)),
        name="peer_experts",
    )(xn, rta, rtb, u_bf, vt_bf, x1, fw)


def _pad_rows(a, n):
    return jnp.pad(a, ((0, n - a.shape[0]),) + ((0, 0),) * (a.ndim - 1))


def kernel(x_prompt, x_sample, cache_conf, cache_mconv, state_ssm, meta_tokens, norm1_w, w_in, b_in, conf_dw_w, conf_dw_b, conf_ln_g, conf_ln_b, w_conf_out, b_conf_out, m_conv_w, m_conv_b, dt_bias, A_log, D_skip, m_norm_w, w_m_out, w_o, norm2_w, peer_w_q, peer_keys, peer_u, peer_v, final_norm_w):
    bp, sp, d = x_prompt.shape
    db, ds, _ = x_sample.shape
    depth = norm1_w.shape[0]
    c_conv = conf_dw_w.shape[2]
    nheads = dt_bias.shape[1]
    di = nheads * M_HEADDIM
    gn = M_GROUPS * M_STATE
    cd = di + 2 * gn
    n_p, n_s = bp * sp, db * ds
    n_all = -(-(n_p + n_s + META_ROWS) // ROW_ALIGN) * ROW_ALIGN
    off_s, off_m = n_p, n_p + n_s

    cols = {"glu_a": 0, "glu_b": c_conv, "z": 2 * c_conv, "xbc": 2 * c_conv + di}
    assert cols["z"] % di == 0 and cols["xbc"] % cd == 0
    cols["gate"] = cols["xbc"] + cd
    assert cols["gate"] % (2 * d) == 0
    cols["dt"] = cols["gate"] + 2 * d
    ncol = cols["dt"] + LANES
    src_dt = 2 * c_conv + di + cd
    src_gate = src_dt + nheads
    tn = ncol // 9 if (ncol % 9 == 0 and (ncol // 9) % LANES == 0) else LANES
    lin_tm = LIN_ROWS if n_all % LIN_ROWS == 0 else ROW_ALIGN

    seq_nb = 8
    prompt_ls = 128
    assert sp % prompt_ls == 0 and db % seq_nb == 0 and n_p % LANES == 0 and n_s % LANES == 0

    x_all = jnp.concatenate([
        x_prompt.reshape(n_p, d), x_sample.reshape(n_s, d), meta_tokens.astype(F32),
        jnp.zeros((n_all - n_p - n_s - N_META, d), F32)], axis=0)

    n_mblk = (n_all - off_m) // META_ROWS
    zc = jnp.zeros((1, CONF_HIST, c_conv), F32)
    zm = jnp.zeros((1, MCONV_HIST, cd), F32)
    zh = jnp.zeros((1, 1, nheads, M_HEADDIM, M_STATE), F32)
    peer_tt = PEER_ROWS if n_all % PEER_ROWS == 0 else ROUTE_ROWS
    zy = lambda: jnp.zeros((n_all, d), F32)
    ssm_p = jnp.zeros((depth, bp, nheads, M_HEADDIM, M_STATE), F32)
    ssm_s = jnp.zeros((depth, db, nheads, M_HEADDIM, M_STATE), F32)
    outs = {k: [] for k in ("conf_p", "mconv_p", "conf_s", "mconv_s")}
    for l in range(depth):
        wl, bl = w_in[l], b_in[l]
        w_re = jnp.concatenate([wl[:, :src_dt], wl[:, src_gate:], wl[:, src_dt:src_gate],
                                jnp.zeros((d, LANES - nheads), F32)], axis=1).astype(BF16)
        b_re = jnp.concatenate([bl[:src_dt], bl[src_gate:], bl[src_dt:src_gate],
                                jnp.zeros((LANES - nheads,), F32)])[None, :]
        proj = _norm_linear(x_all, norm1_w[l][None, :], w_re, b_re, lin_tm, tn)

        pc = {
            "dww": _pad_rows(conf_dw_w[l], CONF_HIST), "dwb": conf_dw_b[l][None, :],
            "lng": conf_ln_g[l][None, :], "lnb": conf_ln_b[l][None, :],
            "wout": w_conf_out[l].astype(BF16), "bout": b_conf_out[l][None, :],
        }
        ps = {
            "mcw": _pad_rows(m_conv_w[l], MCONV_HIST), "mcb": m_conv_b[l][None, :],
            "dtb": jnp.pad(dt_bias[l], (0, LANES - nheads))[None, :],
            "alog": jnp.pad(A_log[l], (0, LANES - nheads))[None, :],
            "dskip": jnp.repeat(D_skip[l], M_HEADDIM)[None, :],
            "mnw": m_norm_w[l][None, :], "wmout": w_m_out[l].astype(BF16),
        }

        ya, cf_m = _conf_branch(proj, zc, pc, zy(), row_off=off_m, nseq=n_mblk, nb=1, ls=META_ROWS,
                                lv=N_META, n_tiles=1, cache_bcast=True)
        yb, h_m, mc_m = _ssd_branch(proj, zm, zh, ps, cols, zy(), None, h0_layer=0, out_layer=0,
                                    out_depth=1, row_off=off_m, nseq=n_mblk, nb=1, ls=META_ROWS,
                                    lv=N_META, n_chunks=1, state_bcast=True)
        ya, cf_p = _conf_branch(proj, cf_m, pc, ya, row_off=0, nseq=bp, nb=1, ls=prompt_ls,
                                lv=prompt_ls, n_tiles=sp // prompt_ls, cache_bcast=True)
        yb, ssm_p, mc_p = _ssd_branch(proj, mc_m, h_m, ps, cols, yb, ssm_p, h0_layer=0, out_layer=l,
                                      out_depth=depth, row_off=0, nseq=bp, nb=1, ls=prompt_ls,
                                      lv=prompt_ls, n_chunks=sp // prompt_ls, state_bcast=True)
        cf_in = jnp.pad(cache_conf[l], ((0, 0), (CONF_HIST - (CONF_W - 1), 0), (0, 0)))
        mc_in = jnp.pad(cache_mconv[l], ((0, 0), (MCONV_HIST - (M_CONV_W - 1), 0), (0, 0)))
        ya, cf_s = _conf_branch(proj, cf_in, pc, ya, row_off=off_s, nseq=db, nb=seq_nb, ls=ds, lv=ds,
                                n_tiles=1, cache_bcast=False)
        yb, ssm_s, mc_s = _ssd_branch(proj, mc_in, state_ssm, ps, cols, yb, ssm_s, h0_layer=l,
                                      out_layer=l, out_depth=depth, row_off=off_s, nseq=db,
                                      nb=seq_nb, ls=ds, lv=ds, n_chunks=1, state_bcast=False)

        pm = {"wo": w_o[l].astype(BF16), "n2": norm2_w[l][None, :], "wq": peer_w_q[l].astype(BF16),
              "keys": peer_keys[l].astype(BF16)}
        x1, xn2, rta, rtb = _merge_route(x_all, ya, yb, proj, pm, cols, ROUTE_ROWS)
        last = l == depth - 1
        x_all = _peer(xn2, rta, rtb, peer_u[l].astype(BF16), peer_v[l].T.astype(BF16), x1,
                      final_norm_w[None, :], peer_tt, PEER_KEY_ROWS * peer_keys.shape[3], last,
                      split_rows=n_p if (last and n_p % peer_tt == 0) else None)

        outs["conf_p"].append(cf_p[:, CONF_HIST - (CONF_W - 1):])
        outs["mconv_p"].append(mc_p[:, MCONV_HIST - (M_CONV_W - 1):])
        outs["conf_s"].append(cf_s[:, CONF_HIST - (CONF_W - 1):])
        outs["mconv_s"].append(mc_s[:, MCONV_HIST - (M_CONV_W - 1):])

    if isinstance(x_all, (list, tuple)):
        y_prompt = x_all[0].reshape(bp, sp, d)
        y_sample = x_all[1][:n_s].reshape(db, ds, d)
    else:
        y_prompt = x_all[:n_p].reshape(bp, sp, d)
        y_sample = x_all[off_s:off_s + n_s].reshape(db, ds, d)
    return (y_prompt, y_sample,
            jnp.stack(outs["conf_p"]), jnp.stack(outs["mconv_p"]), ssm_p,
            jnp.stack(outs["conf_s"]), jnp.stack(outs["mconv_s"]), ssm_s)
```

```python
import functools
import math

import jax
import jax.numpy as jnp
from jax import lax
from jax.experimental import pallas as pl
from jax.experimental.pallas import tpu as pltpu

F32 = jnp.float32
BF16 = jnp.bfloat16

EPS = 1e-6
N_META = 16
CONF_W = 31
M_CONV_W = 4
M_HEADDIM = 64
M_GROUPS = 8
M_STATE = 128
PEER_TOPK = 16
LANES = 128
SUBLANES = 8
BF16_ROWS = 16
CONF_HIST = 32
MCONV_HIST = 8
META_ROWS = 128
ROW_ALIGN = 512
LIN_ROWS = 1280
ROUTE_ROWS = 256
PEER_ROWS = 512
PEER_KEY_ROWS = 8
VMEM_LIMIT = 56 * 1024 * 1024


def _cparams(sem):
    return pltpu.CompilerParams(dimension_semantics=sem, vmem_limit_bytes=VMEM_LIMIT)


def _split3(a):
    p1 = a.astype(BF16)
    r1 = a - p1.astype(F32)
    p2 = r1.astype(BF16)
    r2 = r1 - p2.astype(F32)
    p3 = r2.astype(BF16)
    return p1, p2, p3


def _dot(a, b):
    return jnp.dot(a, b, preferred_element_type=F32)


def _dot_nt(a, b):
    return lax.dot_general(a, b, (((1,), (1,)), ((), ())), preferred_element_type=F32)


def _dot_tn(a, b):
    return lax.dot_general(a, b, (((0,), (0,)), ((), ())), preferred_element_type=F32)


def _sel_right(m01, a):
    p1, p2, p3 = _split3(a)
    return _dot(m01, p1) + _dot(m01, p2) + _dot(m01, p3)


def _sel_left(a, m01):
    p1, p2, p3 = _split3(a)
    return _dot(p1, m01) + _dot(p2, m01) + _dot(p3, m01)


def _transpose_id(a, n):
    eye = (lax.broadcasted_iota(jnp.int32, (n, n), 0)
           == lax.broadcasted_iota(jnp.int32, (n, n), 1)).astype(BF16)
    p1, p2, p3 = _split3(a)
    return _dot_nt(eye, p1) + _dot_nt(eye, p2) + _dot_nt(eye, p3)


def _sigmoid(x):
    return 1.0 / (1.0 + jnp.exp(-x))


def _lin_kernel(x_ref, nw_ref, w_ref, b_ref, o_ref, xn_ref):
    @pl.when(pl.program_id(1) == 0)
    def _():
        x = x_ref[...]
        ms = jnp.mean(x * x, axis=-1, keepdims=True)
        xn_ref[...] = (x * lax.rsqrt(ms + EPS) * nw_ref[...]).astype(BF16)

    o_ref[...] = _dot(xn_ref[...], w_ref[...]) + b_ref[...]


def _norm_linear(x, nw, w, b, tm, tn):
    n, d = x.shape
    nc = w.shape[1]
    return pl.pallas_call(
        _lin_kernel,
        grid=(n // tm, nc // tn),
        in_specs=[
            pl.BlockSpec((tm, d), lambda i, j: (i, 0)),
            pl.BlockSpec((1, d), lambda i, j: (0, 0)),
            pl.BlockSpec((d, tn), lambda i, j: (0, j)),
            pl.BlockSpec((1, tn), lambda i, j: (0, j)),
        ],
        out_specs=pl.BlockSpec((tm, tn), lambda i, j: (i, j)),
        out_shape=jax.ShapeDtypeStruct((n, nc), F32),
        scratch_shapes=[pltpu.VMEM((tm, d), BF16)],
        compiler_params=_cparams(("parallel", "arbitrary")),
        name="norm_linear",
    )(x, nw, w, b)


def _conf_kernel(ga_ref, gb_ref, cache_ref, dww_ref, dwb_ref, lng_ref, lnb_ref, wout_ref, bout_ref,
                 ya_ref, hist_ref, win_ref, c_scr, *, nb, ls, lv):
    @pl.when(pl.program_id(1) == 0)
    def _():
        win_ref[:, 0:CONF_HIST, :] = cache_ref[...]

    u = ga_ref[...] * _sigmoid(gb_ref[...])
    for j in range(nb):
        win_ref[j, CONF_HIST:CONF_HIST + ls, :] = u[j * ls:(j + 1) * ls, :]

    off = CONF_HIST - (CONF_W - 1)
    for j in range(nb):
        for cb in range(dwb_ref.shape[1] // LANES):
            cs = slice(cb * LANES, (cb + 1) * LANES)
            acc = jnp.broadcast_to(dwb_ref[:, cs], (ls, LANES))
            for ph in range(SUBLANES):
                taps = [k for k in range(CONF_W) if (k + off) % SUBLANES == ph]
                if not taps:
                    continue
                span = max((k + off) // SUBLANES for k in taps) * SUBLANES + ls
                if ph:
                    view = pltpu.roll(win_ref[j, 0:span + SUBLANES, cs], span + SUBLANES - ph, 0)
                for k in taps:
                    m = (k + off) // SUBLANES * SUBLANES
                    rows = view[m:m + ls, :] if ph else win_ref[j, m:m + ls, cs]
                    acc = acc + rows * dww_ref[k:k + 1, cs]
            c_scr[j * ls:(j + 1) * ls, cs] = acc
    c = c_scr[...]

    mu = jnp.mean(c, axis=-1, keepdims=True)
    xc = c - mu
    y = xc * lax.rsqrt(jnp.mean(xc * xc, axis=-1, keepdims=True) + EPS)
    y = y * lng_ref[...] + lnb_ref[...]
    y = y * _sigmoid(y)
    ya_ref[...] = (_dot(y.astype(BF16), wout_ref[...]) + bout_ref[...]).astype(BF16)

    for j in range(nb):
        tail = win_ref[j, lv:lv + CONF_HIST, :]
        win_ref[j, 0:CONF_HIST, :] = tail
    hist_ref[...] = win_ref[:, 0:CONF_HIST, :]


def _skip_ref(kern, pos):
    def wrapped(*refs):
        return kern(*refs[:pos], *refs[pos + 1:])
    return wrapped


def _conf_branch(proj, cache, p, y_prev, *, row_off, nseq, nb, ls, lv, n_tiles, cache_bcast):
    c = p["dww"].shape[1]
    d = p["wout"].shape[1]
    r = nb * ls
    bsteps = nseq // nb
    blk0 = row_off // r
    row_map = lambda b, i: (blk0 + b * n_tiles + i, 0)
    cache_map = (lambda b, i: (0, 0, 0)) if cache_bcast else (lambda b, i: (b, 0, 0))
    const2 = lambda b, i: (0, 0)
    kern = functools.partial(_conf_kernel, nb=nb, ls=ls, lv=lv)
    in_specs = [
        pl.BlockSpec((r, c), row_map),
        pl.BlockSpec((r, c), lambda b, i: (blk0 + b * n_tiles + i, 1)),
        pl.BlockSpec((nb, CONF_HIST, c), cache_map),
        pl.BlockSpec((CONF_HIST, c), const2),
        pl.BlockSpec((1, c), const2),
        pl.BlockSpec((1, c), const2),
        pl.BlockSpec((1, c), const2),
        pl.BlockSpec((c, d), const2),
        pl.BlockSpec((1, d), const2),
    ]
    args = [proj, proj, cache, p["dww"], p["dwb"], p["lng"], p["lnb"], p["wout"], p["bout"]]
    aliases = {}
    if y_prev is not None:
        aliases = {len(args): 0}
        kern = _skip_ref(kern, len(args))
        in_specs.append(pl.BlockSpec(memory_space=pl.ANY))
        args.append(y_prev)
    ya, hist = pl.pallas_call(
        kern,
        grid=(bsteps, n_tiles),
        in_specs=in_specs,
        out_specs=[
            pl.BlockSpec((r, d), row_map),
            pl.BlockSpec((nb, CONF_HIST, c), lambda b, i: (b, 0, 0)),
        ],
        out_shape=[
            jax.ShapeDtypeStruct((proj.shape[0], d), BF16),
            jax.ShapeDtypeStruct((nseq, CONF_HIST, c), F32),
        ],
        scratch_shapes=[pltpu.VMEM((nb, CONF_HIST + ls, c), F32), pltpu.VMEM((r, c), F32)],
        input_output_aliases=aliases,
        compiler_params=_cparams(("parallel", "arbitrary")),
        name="conformer_branch",
    )(*args)
    return ya, hist


def _ssd_kernel(xbc_ref, z_ref, dt_ref, mc_ref, h0_ref, cw_ref, cb_ref, dtb_ref, alog_ref, dsk_ref,
                nw_ref, wout_ref, yb_ref, hout_ref, mcout_ref, cwin_scr, y_scr, xc_scr,
                *, nb, ls, lv, nheads, hdim, ngroups, nstate):
    r = nb * ls
    di = nheads * hdim
    gn = ngroups * nstate
    hpg = nheads // ngroups
    gw = hpg * hdim
    ls_shift = int(math.log2(ls))
    i = pl.program_id(1)

    @pl.when(i == 0)
    def _():
        hout_ref[...] = h0_ref[...]
        cwin_scr[:, 0:MCONV_HIST, :] = mc_ref[...]

    xbc = xbc_ref[...]
    for j in range(nb):
        cwin_scr[j, MCONV_HIST:MCONV_HIST + ls, :] = xbc[j * ls:(j + 1) * ls, :]
    off = MCONV_HIST - (M_CONV_W - 1)
    for j in range(nb):
        for cblk in range(cb_ref.shape[1] // LANES):
            cs = slice(cblk * LANES, (cblk + 1) * LANES)
            base = cwin_scr[j, :, cs]
            acc = jnp.broadcast_to(cb_ref[:, cs], (ls, LANES))
            for k in range(M_CONV_W):
                o = off + k
                if o % SUBLANES == 0:
                    rows = base[o:o + ls, :]
                else:
                    rows = pltpu.roll(base, MCONV_HIST + ls - o, 0)[0:ls, :]
                acc = acc + rows * cw_ref[k:k + 1, cs]
            xc_scr[j * ls:(j + 1) * ls, cs] = acc * _sigmoid(acc)
    xc = xc_scr[...]
    for j in range(nb):
        tail = cwin_scr[j, lv:lv + MCONV_HIST, :]
        cwin_scr[j, 0:MCONV_HIST, :] = tail
    mcout_ref[...] = cwin_scr[:, 0:MCONV_HIST, :]

    xs = xc[:, :di]
    bm = xc[:, di:di + gn].astype(BF16)
    cm = xc[:, di + gn:].astype(BF16)

    row1 = lax.broadcasted_iota(jnp.int32, (r, 1), 0)
    valid = (row1 & (ls - 1)) < lv
    dt = jax.nn.softplus(dt_ref[...] + dtb_ref[...])
    dt = jnp.where(valid, dt, 0.0)
    a = dt * (-jnp.exp(alog_ref[...]))

    ri = lax.broadcasted_iota(jnp.int32, (r, r), 0)
    ci = lax.broadcasted_iota(jnp.int32, (r, r), 1)
    same = (ri >> ls_shift) == (ci >> ls_shift)
    causal = same & (ci <= ri)
    acs = _sel_right(causal.astype(BF16), a)
    tot = _sel_right(same.astype(BF16), a)
    acs_t = _transpose_id(acs, LANES)
    etot_t = jnp.exp(_transpose_id(tot, LANES))

    expand = (lax.broadcasted_iota(jnp.int32, (LANES, di), 0)
              == (lax.broadcasted_iota(jnp.int32, (LANES, di), 1) >> int(math.log2(hdim)))).astype(BF16)
    dt_rep = _sel_left(dt, expand)
    eacs_rep = _sel_left(jnp.exp(acs), expand)
    wend_rep = _sel_left(jnp.exp(tot - acs) * dt, expand)
    dx = (xs * dt_rep).astype(BF16)
    wdx = xs * wend_rep

    lane = lax.broadcasted_iota(jnp.int32, (r, LANES), 1)
    heads_per_blk = LANES // hdim
    for g in range(ngroups):
        bg = bm[:, g * nstate:(g + 1) * nstate]
        cg = cm[:, g * nstate:(g + 1) * nstate]
        cb = _dot_nt(cg, bg)
        for blk in range(hpg // heads_per_blk):
            col0 = g * gw + blk * LANES
            dxb = dx[:, col0:col0 + LANES]
            yacc = None
            for q in range(heads_per_blk):
                h = g * hpg + blk * heads_per_blk + q
                seg = acs[:, h:h + 1] - acs_t[h:h + 1, :]
                dec = jnp.exp(jnp.where(causal, seg, -jnp.inf))
                m = (cb * dec).astype(BF16)
                hm = (lane >= q * hdim) & (lane < (q + 1) * hdim)
                t = _dot(m, jnp.where(hm, dxb, jnp.zeros_like(dxb)))
                yacc = t if yacc is None else yacc + t
            y_scr[:, col0:col0 + LANES] = yacc

    col_l = lax.broadcasted_iota(jnp.int32, (r, LANES), 0)

    def seq_body(j, carry):
        if nb == 1:
            wdx_j = wdx
            mrow = None
        else:
            mrow = (row1 >> ls_shift) == j
            wdx_j = jnp.where(mrow, wdx, 0.0)
        wdx_j = wdx_j.astype(BF16)
        selc = (col_l == (j + 1) * ls - 1).astype(BF16)
        el = _sel_left(etot_t, selc)
        for g in range(ngroups):
            bg = bm[:, g * nstate:(g + 1) * nstate]
            cg = cm[:, g * nstate:(g + 1) * nstate]
            hg = jnp.concatenate([hout_ref[j, g * hpg + q] for q in range(hpg)], axis=0)
            yoff = _dot_nt(cg, hg.astype(BF16)) * eacs_rep[:, g * gw:(g + 1) * gw]
            if mrow is not None:
                yoff = jnp.where(mrow, yoff, 0.0)
            y_scr[:, g * gw:(g + 1) * gw] += yoff
            st = _dot_tn(wdx_j[:, g * gw:(g + 1) * gw], bg)
            for q in range(hpg):
                h = g * hpg + q
                scale = jnp.broadcast_to(el[h:h + 1, :], (hdim, nstate))
                hout_ref[j, h] = scale * hout_ref[j, h] + st[q * hdim:(q + 1) * hdim, :]
        return carry

    if nb == 1:
        seq_body(0, 0)
    else:
        lax.fori_loop(0, nb, seq_body, 0)

    y = y_scr[...] + dsk_ref[...] * xs
    z = z_ref[...]
    y = y * (z * _sigmoid(z))
    outs = []
    gsz = di // M_GROUPS
    for g in range(M_GROUPS):
        blk = y[:, g * gsz:(g + 1) * gsz]
        outs.append(blk * lax.rsqrt(jnp.mean(blk * blk, axis=-1, keepdims=True) + EPS))
    yn = jnp.concatenate(outs, axis=1) * nw_ref[...]
    yb_ref[...] = _dot(yn.astype(BF16), wout_ref[...]).astype(BF16)


def _ssd_branch(proj, mcache, h0, p, cols, y_prev, hs_prev, *, h0_layer, out_layer, out_depth,
                row_off, nseq, nb, ls, lv, n_chunks, state_bcast):
    nheads, hdim, nstate = h0.shape[2], h0.shape[3], h0.shape[4]
    di = nheads * hdim
    cd = p["mcw"].shape[1]
    d = p["wmout"].shape[1]
    r = nb * ls
    bsteps = nseq // nb
    blk0 = row_off // r
    const2 = lambda b, i: (0, 0)
    st_map = ((lambda b, i: (h0_layer, 0, 0, 0, 0)) if state_bcast
              else (lambda b, i: (h0_layer, b, 0, 0, 0)))
    mc_map = (lambda b, i: (0, 0, 0)) if state_bcast else (lambda b, i: (b, 0, 0))
    kern = functools.partial(_ssd_kernel, nb=nb, ls=ls, lv=lv, nheads=nheads, hdim=hdim,
                             ngroups=M_GROUPS, nstate=nstate)
    in_specs = [
        pl.BlockSpec((r, cd), lambda b, i: (blk0 + b * n_chunks + i, cols["xbc"] // cd)),
        pl.BlockSpec((r, di), lambda b, i: (blk0 + b * n_chunks + i, cols["z"] // di)),
        pl.BlockSpec((r, LANES), lambda b, i: (blk0 + b * n_chunks + i, cols["dt"] // LANES)),
        pl.BlockSpec((nb, MCONV_HIST, cd), mc_map),
        pl.BlockSpec((None, nb, nheads, hdim, nstate), st_map),
        pl.BlockSpec((MCONV_HIST, cd), const2),
        pl.BlockSpec((1, cd), const2),
        pl.BlockSpec((1, LANES), const2),
        pl.BlockSpec((1, LANES), const2),
        pl.BlockSpec((1, di), const2),
        pl.BlockSpec((1, di), const2),
        pl.BlockSpec((di, d), const2),
    ]
    args = [proj, proj, proj, mcache, h0, p["mcw"], p["mcb"], p["dtb"], p["alog"], p["dskip"],
            p["mnw"], p["wmout"]]
    aliases = {}
    n_in = len(args)
    for prev, out_idx in ((y_prev, 0), (hs_prev, 1)):
        if prev is not None:
            aliases[len(args)] = out_idx
            in_specs.append(pl.BlockSpec(memory_space=pl.ANY))
            args.append(prev)
    for pos in range(n_in, len(args)):
        kern = _skip_ref(kern, pos)
    yb, hout, mcout = pl.pallas_call(
        kern,
        grid=(bsteps, n_chunks),
        in_specs=in_specs,
        out_specs=[
            pl.BlockSpec((r, d), lambda b, i: (blk0 + b * n_chunks + i, 0)),
            pl.BlockSpec((None, nb, nheads, hdim, nstate), lambda b, i: (out_layer, b, 0, 0, 0)),
            pl.BlockSpec((nb, MCONV_HIST, cd), lambda b, i: (b, 0, 0)),
        ],
        out_shape=[
            jax.ShapeDtypeStruct((proj.shape[0], d), BF16),
            jax.ShapeDtypeStruct((out_depth, nseq, nheads, hdim, nstate), F32),
            jax.ShapeDtypeStruct((nseq, MCONV_HIST, cd), F32),
        ],
        scratch_shapes=[
            pltpu.VMEM((nb, MCONV_HIST + ls, cd), F32),
            pltpu.VMEM((r, di), F32),
            pltpu.VMEM((r, cd), F32),
        ],
        input_output_aliases=aliases,
        compiler_params=_cparams(("parallel", "arbitrary")),
        name="ssd_branch",
    )(*args)
    return yb, hout, mcout


def _topk_rows(s, k):
    n = s.shape[0]
    rows = lax.broadcasted_iota(jnp.int32, s.shape, 0).astype(F32)
    rank = jnp.full(s.shape, float(k), F32)
    vals = []
    for j in range(k):
        m = jnp.max(s, axis=0, keepdims=True)
        idx = jnp.min(jnp.where(s == m, rows, float(n)), axis=0, keepdims=True)
        hit = rows == idx
        rank = jnp.where(hit, float(j), rank)
        s = jnp.where(hit, -jnp.inf, s)
        vals.append(m)
    return vals, rank


def _cand_pairs(k):
    return [(a, j) for a in range(k) for j in range(k // (a + 1))]


def _gather_rows(src, idxs):
    sub = lax.broadcasted_iota(jnp.int32, (SUBLANES, src.shape[1]), 0)
    groups = {}
    for r, ix in enumerate(idxs):
        if ix is not None:
            groups.setdefault((ix // SUBLANES, (r - ix) % SUBLANES), []).append(r)
    out = None
    for (slab, shift), rs in groups.items():
        piece = src[slab * SUBLANES:(slab + 1) * SUBLANES]
        if shift:
            piece = pltpu.roll(piece, shift, 0)
        if out is None:
            out = piece
        else:
            m = functools.reduce(lambda p, q: p | q, [sub == r for r in rs])
            out = jnp.where(m, piece, out)
    return out


def _row_range_sum(x, lo, hi):
    if hi - lo == 1:
        return x[lo:lo + 1]
    sub = lax.broadcasted_iota(jnp.int32, (SUBLANES, x.shape[1]), 0)
    tot = None
    for sl in range(lo // SUBLANES, (hi - 1) // SUBLANES + 1):
        piece = x[sl * SUBLANES:(sl + 1) * SUBLANES]
        r0 = max(lo, sl * SUBLANES) - sl * SUBLANES
        r1 = min(hi, (sl + 1) * SUBLANES) - sl * SUBLANES
        if (r0, r1) != (0, SUBLANES):
            piece = jnp.where((sub >= r0) & (sub < r1), piece, 0.0)
        tot = piece if tot is None else tot + piece
    return jnp.sum(tot, axis=0, keepdims=True)


def _topk_blind(s, k):
    orig = s
    rank = jnp.full(s.shape, float(k), F32)
    vals = []
    for j in range(k):
        m = jnp.max(s, axis=0, keepdims=True)
        hit = s == m
        rank = jnp.where(hit, float(j), rank)
        s = jnp.where(hit, -jnp.inf, s)
        vals.append(m)
    removed = jnp.sum(jnp.where(s != orig, 1.0, 0.0), axis=0, keepdims=True)
    return vals, rank, jnp.abs(removed - float(k))


def _route_head(s1, s2, k, exact):
    if exact:
        v1, rank1 = _topk_rows(s1, k)
        v2, rank2 = _topk_rows(s2, k)
        ties = jnp.zeros((1, s1.shape[1]), F32)
    else:
        v1, rank1, t1 = _topk_blind(s1, k)
        v2, rank2, t2 = _topk_blind(s2, k)
        ties = t1 + t2
    v1m = jnp.concatenate(v1, axis=0)
    v2m = jnp.concatenate(v2, axis=0)
    pairs = _cand_pairs(k)
    nslab = -(-len(pairs) // SUBLANES)
    padded = pairs + [None] * (nslab * SUBLANES - len(pairs))
    slabs = []
    for sl in range(nslab):
        chunk = padded[sl * SUBLANES:(sl + 1) * SUBLANES]
        slabs.append(_gather_rows(v1m, [None if pr is None else pr[0] for pr in chunk])
                     + _gather_rows(v2m, [None if pr is None else pr[1] for pr in chunk]))
    cand = jnp.concatenate(slabs, axis=0)
    rows = lax.broadcasted_iota(jnp.int32, cand.shape, 0).astype(F32)
    cand = jnp.where(rows < float(len(pairs)), cand, -jnp.inf)
    c = cand
    for _ in range(k):
        m = jnp.max(c, axis=0, keepdims=True)
        if exact:
            idx = jnp.min(jnp.where(c == m, rows, float(cand.shape[0])), axis=0, keepdims=True)
            c = jnp.where(rows == idx, -jnp.inf, c)
        else:
            c = jnp.where(c == m, -jnp.inf, c)
    sel = c != cand
    self32 = sel.astype(F32)
    if not exact:
        ties = ties + jnp.abs(jnp.sum(self32, axis=0, keepdims=True) - float(k))
    cmax = v1[0] + v2[0]
    zsum = jnp.sum(jnp.where(sel, jnp.exp(cand - cmax), 0.0), axis=0, keepdims=True)
    cn = jnp.zeros(s1.shape, F32)
    for a in range(k):
        lo = pairs.index((a, 0))
        hi = lo + k // (a + 1)
        cn = jnp.where(rank1 == float(a), _row_range_sum(self32, lo, hi), cn)
    return cn, 0.5 * jnp.exp(s1 - v1[0]) / zsum, rank2, jnp.exp(s2 - v2[0]), ties


def _merge_route_kernel(x_ref, ya_ref, yb_ref, g_ref, wo_ref, n2_ref, wq_ref, keys_ref,
                        x1_ref, xn_ref, rta_ref, rtb_ref, *, nheads, nkeys, dhalf):
    d = x_ref.shape[1]
    g = _sigmoid(g_ref[...])
    mix = g[:, :d] * ya_ref[...].astype(F32) + g[:, d:] * yb_ref[...].astype(F32)
    x1 = x_ref[...] + _dot(mix.astype(BF16), wo_ref[...])
    x1_ref[...] = x1
    ms = jnp.mean(x1 * x1, axis=-1, keepdims=True)
    xn = (x1 * lax.rsqrt(ms + EPS) * n2_ref[...]).astype(BF16)
    xn_ref[...] = xn
    q = _dot(xn, wq_ref[...]).astype(BF16)

    k = PEER_TOPK
    for h in range(nheads):
        s1 = _dot_nt(keys_ref[h, 0], q[:, (2 * h) * dhalf:(2 * h + 1) * dhalf])
        s2 = _dot_nt(keys_ref[h, 1], q[:, (2 * h + 1) * dhalf:(2 * h + 2) * dhalf])

        def emit(exact, h=h, s1=s1, s2=s2):
            ties = None
            for lg in range(s1.shape[1] // LANES):
                ls_ = slice(lg * LANES, (lg + 1) * LANES)
                cn, e1, rank2, e2, t = _route_head(s1[:, ls_], s2[:, ls_], k, exact)
                rta_ref[h, 0, :, ls_] = cn
                rta_ref[h, 1, :, ls_] = e1
                rtb_ref[h, 0, :, ls_] = rank2.astype(BF16)
                rtb_ref[h, 1, :, ls_] = e2.astype(BF16)
                ties = t if ties is None else jnp.maximum(ties, t)
            return ties

        ties = emit(False)

        @pl.when(jnp.max(ties) > 0.0)
        def _():
            emit(True)


def _merge_route(x, ya, yb, proj, p, cols, tm):
    n, d = x.shape
    nheads, _, nkeys, dhalf = p["keys"].shape
    dq = p["wq"].shape[1]
    const2 = lambda i: (0, 0)
    kern = functools.partial(_merge_route_kernel, nheads=nheads, nkeys=nkeys, dhalf=dhalf)
    return pl.pallas_call(
        kern,
        grid=(n // tm,),
        in_specs=[
            pl.BlockSpec((tm, d), lambda i: (i, 0)),
            pl.BlockSpec((tm, d), lambda i: (i, 0)),
            pl.BlockSpec((tm, d), lambda i: (i, 0)),
            pl.BlockSpec((tm, 2 * d), lambda i: (i, cols["gate"] // (2 * d))),
            pl.BlockSpec((d, d), const2),
            pl.BlockSpec((1, d), const2),
            pl.BlockSpec((d, dq), const2),
            pl.BlockSpec((nheads, 2, nkeys, dhalf), lambda i: (0, 0, 0, 0)),
        ],
        out_specs=[
            pl.BlockSpec((tm, d), lambda i: (i, 0)),
            pl.BlockSpec((tm, d), lambda i: (i, 0)),
            pl.BlockSpec((nheads, 2, nkeys, tm), lambda i: (0, 0, 0, i)),
            pl.BlockSpec((nheads, 2, nkeys, tm), lambda i: (0, 0, 0, i)),
        ],
        out_shape=[
            jax.ShapeDtypeStruct((n, d), F32),
            jax.ShapeDtypeStruct((n, d), BF16),
            jax.ShapeDtypeStruct((nheads, 2, nkeys, n), F32),
            jax.ShapeDtypeStruct((nheads, 2, nkeys, n), BF16),
        ],
        compiler_params=_cparams(("parallel",)),
        name="merge_route",
    )(x, ya, yb, proj, p["wo"], p["n2"], p["wq"], p["keys"])


def _peer_kernel(xn_ref, rta_ref, rtb_ref, u_ref, vt_ref, x1_ref, fw_ref, *rest,
                 nheads, nkeys, nblk, final_norm, split_tile):
    if split_tile is None:
        o_ref, o2_ref = rest[0], None
        acc_ref, ht0, ht1, wt0, wt1 = rest[1:]
    else:
        o_ref, o2_ref = rest[0], rest[1]
        acc_ref, ht0, ht1, wt0, wt1 = rest[2:]
    f = pl.program_id(0)
    eb = u_ref.shape[0]
    na = eb // nkeys
    e_b = lax.rem(jnp.maximum(f - 1, 0), nblk)
    e_c = lax.rem(jnp.maximum(f - 2, 0), nblk)

    @pl.when(f == 0)
    def _():
        for r in (ht0, ht1, wt0, wt1):
            r[...] = jnp.zeros_like(r)

    @pl.when(e_c == 0)
    def _():
        acc_ref[...] = jnp.zeros_like(acc_ref)

    def stages(ht_w, ht_r, wt_w, wt_r):
        ht_w[...] = _dot_nt(u_ref[...], xn_ref[...])
        a0 = pl.multiple_of(e_b * na, na)
        slabs = [(rta_ref[h, 0, pl.ds(a0, na), :], rta_ref[h, 1, pl.ds(a0, na), :])
                 for h in range(nheads)]
        tt = xn_ref.shape[0]
        nv = nkeys // BF16_ROWS
        for al in range(na):
            gates = [None] * nv
            for h in range(nheads):
                cn = jnp.broadcast_to(slabs[h][0][al:al + 1, :], (BF16_ROWS, tt)).astype(BF16)
                e1 = jnp.broadcast_to(slabs[h][1][al:al + 1, :], (BF16_ROWS, tt)).astype(BF16)
                for v in range(nv):
                    rk = rtb_ref[h, 0, v * BF16_ROWS:(v + 1) * BF16_ROWS, :]
                    p2 = rtb_ref[h, 1, v * BF16_ROWS:(v + 1) * BF16_ROWS, :]
                    t = jnp.where(rk < cn, p2, jnp.zeros_like(p2)) * e1
                    gates[v] = t if gates[v] is None else gates[v] + t
            for v in range(nv):
                r0 = al * nkeys + v * BF16_ROWS
                hv = ht_r[r0:r0 + BF16_ROWS, :]
                act = hv * (1.0 + lax.erf(hv * (1.0 / math.sqrt(2.0))))
                wt_w[r0:r0 + BF16_ROWS, :] = (gates[v].astype(F32) * act).astype(BF16)
        acc_ref[...] += _dot(vt_ref[...], wt_r[...])

    even = lax.rem(f, 2) == 0

    @pl.when(even)
    def _():
        stages(ht0, ht1, wt1, wt0)

    @pl.when(jnp.logical_not(even))
    def _():
        stages(ht1, ht0, wt0, wt1)

    @pl.when((f >= 2) & (e_c == nblk - 1))
    def _():
        x2 = x1_ref[...] + acc_ref[...].T
        if final_norm:
            ms = jnp.mean(x2 * x2, axis=-1, keepdims=True)
            x2 = x2 * lax.rsqrt(ms + EPS) * fw_ref[...]
        if split_tile is None:
            o_ref[...] = x2
        else:
            tile = (f - 2) // nblk

            @pl.when(tile < split_tile)
            def _():
                o_ref[...] = x2

            @pl.when(tile >= split_tile)
            def _():
                o2_ref[...] = x2


def _peer(xn, rta, rtb, u_bf, vt_bf, x1, fw, tt, eb, final_norm, split_rows=None):
    n, d = x1.shape
    ne = u_bf.shape[0]
    nheads, _, nkeys, _ = rta.shape
    nblk = ne // eb
    nflat = (n // tt) * nblk
    fa = lambda f: jnp.minimum(f, nflat - 1)
    fb = lambda f: jnp.clip(f - 1, 0, nflat - 1)
    fc = lambda f: jnp.clip(f - 2, 0, nflat - 1)
    tile_c = lambda f: fc(f) // nblk
    if split_rows is None:
        split_tile = None
        out_specs = pl.BlockSpec((tt, d), lambda f: (tile_c(f), 0))
        out_shape = jax.ShapeDtypeStruct((n, d), F32)
    else:
        split_tile = split_rows // tt
        out_specs = [
            pl.BlockSpec((tt, d), lambda f: (jnp.minimum(tile_c(f), split_tile - 1), 0)),
            pl.BlockSpec((tt, d), lambda f: (jnp.maximum(tile_c(f) - split_tile, 0), 0)),
        ]
        out_shape = [jax.ShapeDtypeStruct((split_rows, d), F32),
                     jax.ShapeDtypeStruct((n - split_rows, d), F32)]
    kern = functools.partial(_peer_kernel, nheads=nheads, nkeys=nkeys, nblk=nblk,
                             final_norm=final_norm, split_tile=split_tile)
    return pl.pallas_call(
        kern,
        grid=(nflat + 2,),
        in_specs=[
            pl.BlockSpec((tt, d), lambda f: (fa(f) // nblk, 0)),
            pl.BlockSpec((nheads, 2, nkeys, tt), lambda f: (0, 0, 0, fb(f) // nblk)),
            pl.BlockSpec((nheads, 2, nkeys, tt), lambda f: (0, 0, 0, fb(f) // nblk)),
            pl.BlockSpec((eb, d), lambda f: (fa(f) % nblk, 0)),
            pl.BlockSpec((d, eb), lambda f: (0, fc(f) % nblk)),
            pl.BlockSpec((tt, d), lambda f: (fc(f) // nblk, 0)),
            pl.BlockSpec((1, d), lambda f: (0, 0)),
        ],
        out_specs=out_specs,
        out_shape=out_shape,
        scratch_shapes=[
            pltpu.VMEM((d, tt), F32),
            pltpu.VMEM((eb, tt), F32), pltpu.VMEM((eb, tt), F32),
            pltpu.VMEM((eb, tt), BF16), pltpu.VMEM((eb, tt), BF16),
        ],
        compiler_params=_cparams(("arbitrary",)),
        name="peer_experts",
    )(xn, rta, rtb, u_bf, vt_bf, x1, fw)


def _pad_rows(a, n):
    return jnp.pad(a, ((0, n - a.shape[0]),) + ((0, 0),) * (a.ndim - 1))


def kernel(x_prompt, x_sample, cache_conf, cache_mconv, state_ssm, meta_tokens, norm1_w, w_in, b_in, conf_dw_w, conf_dw_b, conf_ln_g, conf_ln_b, w_conf_out, b_conf_out, m_conv_w, m_conv_b, dt_bias, A_log, D_skip, m_norm_w, w_m_out, w_o, norm2_w, peer_w_q, peer_keys, peer_u, peer_v, final_norm_w):
    bp, sp, d = x_prompt.shape
    db, ds, _ = x_sample.shape
    depth = norm1_w.shape[0]
    c_conv = conf_dw_w.shape[2]
    nheads = dt_bias.shape[1]
    di = nheads * M_HEADDIM
    gn = M_GROUPS * M_STATE
    cd = di + 2 * gn
    n_p, n_s = bp * sp, db * ds
    n_all = -(-(n_p + n_s + META_ROWS) // ROW_ALIGN) * ROW_ALIGN
    off_s, off_m = n_p, n_p + n_s

    cols = {"glu_a": 0, "glu_b": c_conv, "z": 2 * c_conv, "xbc": 2 * c_conv + di}
    assert cols["z"] % di == 0 and cols["xbc"] % cd == 0
    cols["gate"] = cols["xbc"] + cd
    assert cols["gate"] % (2 * d) == 0
    cols["dt"] = cols["gate"] + 2 * d
    ncol = cols["dt"] + LANES
    src_dt = 2 * c_conv + di + cd
    src_gate = src_dt + nheads
    tn = ncol // 9 if (ncol % 9 == 0 and (ncol // 9) % LANES == 0) else LANES
    lin_tm = LIN_ROWS if n_all % LIN_ROWS == 0 else ROW_ALIGN

    seq_nb = 8
    prompt_ls = 128
    assert sp % prompt_ls == 0 and db % seq_nb == 0 and n_p % LANES == 0 and n_s % LANES == 0

    x_all = jnp.concatenate([
        x_prompt.reshape(n_p, d), x_sample.reshape(n_s, d), meta_tokens.astype(F32),
        jnp.zeros((n_all - n_p - n_s - N_META, d), F32)], axis=0)

    n_mblk = (n_all - off_m) // META_ROWS
    zc = jnp.zeros((1, CONF_HIST, c_conv), F32)
    zm = jnp.zeros((1, MCONV_HIST, cd), F32)
    zh = jnp.zeros((1, 1, nheads, M_HEADDIM, M_STATE), F32)
    peer_tt = PEER_ROWS if n_all % PEER_ROWS == 0 else ROUTE_ROWS
    zy = lambda: jnp.zeros((n_all, d), BF16)
    ssm_p = jnp.zeros((depth, bp, nheads, M_HEADDIM, M_STATE), F32)
    ssm_s = jnp.zeros((depth, db, nheads, M_HEADDIM, M_STATE), F32)
    outs = {k: [] for k in ("conf_p", "mconv_p", "conf_s", "mconv_s")}
    for l in range(depth):
        wl, bl = w_in[l], b_in[l]
        w_re = jnp.concatenate([wl[:, :src_dt], wl[:, src_gate:], wl[:, src_dt:src_gate],
                                jnp.zeros((d, LANES - nheads), F32)], axis=1).astype(BF16)
        b_re = jnp.concatenate([bl[:src_dt], bl[src_gate:], bl[src_dt:src_gate],
                                jnp.zeros((LANES - nheads,), F32)])[None, :]
        proj = _norm_linear(x_all, norm1_w[l][None, :], w_re, b_re, lin_tm, tn)

        pc = {
            "dww": _pad_rows(conf_dw_w[l], CONF_HIST), "dwb": conf_dw_b[l][None, :],
            "lng": conf_ln_g[l][None, :], "lnb": conf_ln_b[l][None, :],
            "wout": w_conf_out[l].astype(BF16), "bout": b_conf_out[l][None, :],
        }
        ps = {
            "mcw": _pad_rows(m_conv_w[l], MCONV_HIST), "mcb": m_conv_b[l][None, :],
            "dtb": jnp.pad(dt_bias[l], (0, LANES - nheads))[None, :],
            "alog": jnp.pad(A_log[l], (0, LANES - nheads))[None, :],
            "dskip": jnp.repeat(D_skip[l], M_HEADDIM)[None, :],
            "mnw": m_norm_w[l][None, :], "wmout": w_m_out[l].astype(BF16),
        }

        ya, cf_m = _conf_branch(proj, zc, pc, zy(), row_off=off_m, nseq=n_mblk, nb=1, ls=META_ROWS,
                                lv=N_META, n_tiles=1, cache_bcast=True)
        yb, h_m, mc_m = _ssd_branch(proj, zm, zh, ps, cols, zy(), None, h0_layer=0, out_layer=0,
                                    out_depth=1, row_off=off_m, nseq=n_mblk, nb=1, ls=META_ROWS,
                                    lv=N_META, n_chunks=1, state_bcast=True)
        ya, cf_p = _conf_branch(proj, cf_m, pc, ya, row_off=0, nseq=bp, nb=1, ls=prompt_ls,
                                lv=prompt_ls, n_tiles=sp // prompt_ls, cache_bcast=True)
        yb, ssm_p, mc_p = _ssd_branch(proj, mc_m, h_m, ps, cols, yb, ssm_p, h0_layer=0, out_layer=l,
                                      out_depth=depth, row_off=0, nseq=bp, nb=1, ls=prompt_ls,
                                      lv=prompt_ls, n_chunks=sp // prompt_ls, state_bcast=True)
        cf_in = jnp.pad(cache_conf[l], ((0, 0), (CONF_HIST - (CONF_W - 1), 0), (0, 0)))
        mc_in = jnp.pad(cache_mconv[l], ((0, 0), (MCONV_HIST - (M_CONV_W - 1), 0), (0, 0)))
        ya, cf_s = _conf_branch(proj, cf_in, pc, ya, row_off=off_s, nseq=db, nb=seq_nb, ls=ds, lv=ds,
                                n_tiles=1, cache_bcast=False)
        yb, ssm_s, mc_s = _ssd_branch(proj, mc_in, state_ssm, ps, cols, yb, ssm_s, h0_layer=l,
                                      out_layer=l, out_depth=depth, row_off=off_s, nseq=db,
                                      nb=seq_nb, ls=ds, lv=ds, n_chunks=1, state_bcast=False)

        pm = {"wo": w_o[l].astype(BF16), "n2": norm2_w[l][None, :], "wq": peer_w_q[l].astype(BF16),
              "keys": peer_keys[l].astype(BF16)}
        x1, xn2, rta, rtb = _merge_route(x_all, ya, yb, proj, pm, cols, ROUTE_ROWS)
        last = l == depth - 1
        x_all = _peer(xn2, rta, rtb, peer_u[l].astype(BF16), peer_v[l].astype(BF16).T, x1,
                      final_norm_w[None, :], peer_tt, PEER_KEY_ROWS * peer_keys.shape[3], last,
                      split_rows=n_p if (last and n_p % peer_tt == 0) else None)

        outs["conf_p"].append(cf_p[:, CONF_HIST - (CONF_W - 1):])
        outs["mconv_p"].append(mc_p[:, MCONV_HIST - (M_CONV_W - 1):])
        outs["conf_s"].append(cf_s[:, CONF_HIST - (CONF_W - 1):])
        outs["mconv_s"].append(mc_s[:, MCONV_HIST - (M_CONV_W - 1):])

    if isinstance(x_all, (list, tuple)):
        y_prompt = x_all[0].reshape(bp, sp, d)
        y_sample = x_all[1][:n_s].reshape(db, ds, d)
    else:
        y_prompt = x_all[:n_p].reshape(bp, sp, d)
        y_sample = x_all[off_s:off_s + n_s].reshape(db, ds, d)
    return (y_prompt, y_sample,
            jnp.stack(outs["conf_p"]), jnp.stack(outs["mconv_p"]), ssm_p,
            jnp.stack(outs["conf_s"]), jnp.stack(outs["mconv_s"]), ssm_s)
```
